```python
import math
import jax
import jax.numpy as jnp
from jax import lax
import numpy as np

D_MODEL = 1024
BATCH = 8
SEQ = 4096
DEPTH = 2

CTX_LEN = 256
GRID_W = 64
HEAD_DIM = 64
ROPE_BASE = 10000.0
Q_BLOCK = 128
EPS = 1e-6
N_MOD = 9
N_BRANCH = 4
BRANCH_W = D_MODEL // 2
A_Q_HEADS = BRANCH_W // HEAD_DIM
A_KV_HEADS = 2
A_GROUP = A_Q_HEADS // A_KV_HEADS
B_CH = BRANCH_W
CONV_W = 31
C_HEADS = BRANCH_W // (2 * HEAD_DIM)
C_V_DIM = 2 * HEAD_DIM
D_GROUPS = 4
D_GROUP_CH = BRANCH_W // D_GROUPS
D_FF = 256 * ((8 * D_MODEL // 3 + 255) // 256)
ATTN_SCALE = HEAD_DIM ** -0.5
IN_SIZES = (A_Q_HEADS * HEAD_DIM, A_KV_HEADS * HEAD_DIM, A_KV_HEADS * HEAD_DIM,
            2 * B_CH,
            C_HEADS * 2 * HEAD_DIM, C_HEADS * 2 * HEAD_DIM, C_HEADS * C_V_DIM,
            D_GROUPS * D_GROUP_CH,
            N_BRANCH * D_MODEL)
IN_COLS = sum(IN_SIZES)

kernel_name = 'hybrid_prefix_dit_block'


def rms_norm(x, g):
    xf = x.astype(jnp.float32)
    y = xf * lax.rsqrt(jnp.mean(xf * xf, axis=-1, keepdims=True) + EPS)
    return (y * g.astype(jnp.float32)).astype(x.dtype)


def layer_norm(x, g, b):
    xf = x.astype(jnp.float32)
    mu = jnp.mean(xf, axis=-1, keepdims=True)
    xc = xf - mu
    var = jnp.mean(xc * xc, axis=-1, keepdims=True)
    return (xc * lax.rsqrt(var + EPS) * g.astype(jnp.float32) + b.astype(jnp.float32)).astype(x.dtype)


def modulate(x, shift, scale):
    return x * (1 + scale) + shift


def adaln(cond, w, b):
    m = jax.nn.silu(cond) @ w + b
    return m.reshape(cond.shape[0], N_MOD, 1, D_MODEL)


def swiglu(x, w1, w3, w2):
    return (jax.nn.silu(x @ w1) * (x @ w3)) @ w2


def half_ffn(h, shift, scale, gate, g, w1, w3, w2):
    n = modulate(rms_norm(h, g), shift, scale)
    return h + 0.5 * gate * swiglu(n, w1, w3, w2)


def split_cols(z):
    offs, acc = [], 0
    for s in IN_SIZES[:-1]:
        acc += s
        offs.append(acc)
    return jnp.split(z, offs, axis=-1)


def axial_rope_tables(n_tokens):
    rows = n_tokens // GRID_W
    row = jnp.broadcast_to(jnp.arange(rows)[:, None], (rows, GRID_W)).reshape(-1).astype(jnp.float32)
    col = jnp.broadcast_to(jnp.arange(GRID_W)[None, :], (rows, GRID_W)).reshape(-1).astype(jnp.float32)
    half = HEAD_DIM // 2
    inv = ROPE_BASE ** (-jnp.arange(0, half, 2, dtype=jnp.float32) / half)
    ang_r = row[:, None] * inv
    ang_c = col[:, None] * inv
    ang = jnp.concatenate([ang_r, ang_r, ang_c, ang_c], axis=-1)
    return jnp.cos(ang), jnp.sin(ang)


def apply_rope(x, cos, sin):
    shape = (x.shape[1],) + (1,) * (x.ndim - 3) + (HEAD_DIM,)
    cos, sin = cos.reshape(shape), sin.reshape(shape)
    x1, x2, x3, x4 = jnp.split(x, 4, axis=-1)
    rot = jnp.concatenate([-x2, x1, -x4, x3], axis=-1)
    return (x * cos + rot * sin).astype(x.dtype)


def blockwise(fn, q, k, v):
    b, s = q.shape[:2]
    nb = s // Q_BLOCK
    qb = jnp.moveaxis(q.reshape((b, nb, Q_BLOCK) + q.shape[2:]), 1, 0)
    out = lax.map(lambda qi: fn(qi, k, v), qb)
    out = jnp.moveaxis(out, 0, 1)
    return out.reshape((b, s) + out.shape[3:])


def gqa_heads(aq, ak, av, qk_g):
    b, t = aq.shape[:2]
    q = rms_norm(aq.reshape(b, t, A_KV_HEADS, A_GROUP, HEAD_DIM), qk_g[0])
    k = rms_norm(ak.reshape(b, t, A_KV_HEADS, HEAD_DIM), qk_g[1])
    v = av.reshape(b, t, A_KV_HEADS, HEAD_DIM)
    return q, k, v


def gqa_attend(q, k, v):
    s = jnp.einsum('bqkgd,btkd->bkgqt', q, k).astype(jnp.float32) * ATTN_SCALE
    p = jax.nn.softmax(s, axis=-1).astype(v.dtype)
    return jnp.einsum('bkgqt,btkd->bqkgd', p, v)


def diff_heads(cq, ck, cv):
    b, t = cq.shape[:2]
    q = cq.reshape(b, t, C_HEADS, 2, HEAD_DIM)
    k = ck.reshape(b, t, C_HEADS, 2, HEAD_DIM)
    v = cv.reshape(b, t, C_HEADS, C_V_DIM)
    return q, k, v


def diff_attend(q, k, v, lam):
    s = jnp.einsum('bqhpd,bthpd->bhpqt', q, k).astype(jnp.float32) * ATTN_SCALE
    p = jax.nn.softmax(s, axis=-1)
    w = (p[:, :, 0] - lam * p[:, :, 1]).astype(v.dtype)
    return jnp.einsum('bhqt,bthe->bqhe', w, v)


def diff_out(o, g, lam_init):
    o = rms_norm(o, g) * (1.0 - lam_init)
    return o.reshape(o.shape[:2] + (BRANCH_W,))


def conv_module(u, conv_w, conv_b, ln_g, ln_b):
    a, g = jnp.split(u, 2, axis=-1)
    y = a * jax.nn.sigmoid(g)
    y = lax.conv_general_dilated(y, conv_w[:, None, :], window_strides=(1,),
                                 padding=[(CONV_W // 2, CONV_W // 2)],
                                 dimension_numbers=('NWC', 'WIO', 'NWC'),
                                 feature_group_count=B_CH) + conv_b
    return jax.nn.silu(layer_norm(y, ln_g, ln_b))


def fourier_mix(u):
    b, t = u.shape[:2]
    z = u.astype(jnp.float32).reshape(b, t, D_GROUPS, D_GROUP_CH)
    f = jnp.fft.fft2(z, axes=(1, 3), norm='ortho').real
    return f.reshape(b, t, BRANCH_W).astype(u.dtype)


def merge(y_a, y_b, y_c, y_d, gate_logits, w_branch, w_out):
    y = jnp.stack([y_a, y_b, y_c, y_d], axis=2)
    proj = jnp.einsum('btnc,ncd->btnd', y, w_branch)
    g = jax.nn.sigmoid(gate_logits.reshape(gate_logits.shape[:2] + (N_BRANCH, D_MODEL)))
    return jnp.einsum('btnd,btnd->btd', g, proj) @ w_out


def token_mixing(n, nc, w_in, qk_norm_a, conv_w, conv_b, conv_ln_g, conv_ln_b,
                 diff_lam, diff_subln_g, w_branch, w_out, lam_init, cos, sin, update_ctx):
    aq, ak, av, bu, cq, ck, cv, du, gl = split_cols(n @ w_in)
    aqc, akc, avc, buc, cqc, ckc, cvc, duc, glc = split_cols(nc @ w_in)
    lp = diff_lam.astype(jnp.float32)
    lam = jnp.exp(jnp.sum(lp[0] * lp[1])) - jnp.exp(jnp.sum(lp[2] * lp[3])) + lam_init

    qa, ka, va = gqa_heads(aq, ak, av, qk_norm_a)
    qa, ka = apply_rope(qa, cos, sin), apply_rope(ka, cos, sin)
    qac, kac, vac = gqa_heads(aqc, akc, avc, qk_norm_a)
    y_a = blockwise(gqa_attend, qa, jnp.concatenate([ka, kac], axis=1), jnp.concatenate([va, vac], axis=1))
    y_a = y_a.reshape(y_a.shape[:2] + (BRANCH_W,))

    qd, kd, vd = diff_heads(cq, ck, cv)
    qd, kd = apply_rope(qd, cos, sin), apply_rope(kd, cos, sin)
    qdc, kdc, vdc = diff_heads(cqc, ckc, cvc)
    y_c = blockwise(lambda qi, k, v: diff_attend(qi, k, v, lam), qd,
                    jnp.concatenate([kd, kdc], axis=1), jnp.concatenate([vd, vdc], axis=1))
    y_c = diff_out(y_c, diff_subln_g, lam_init)

    y_b = conv_module(bu, conv_w, conv_b, conv_ln_g, conv_ln_b)
    y_d = fourier_mix(du)
    y = merge(y_a, y_b, y_c, y_d, gl, w_branch, w_out)
    if not update_ctx:
        return y, None

    y_a_ctx = gqa_attend(qac, kac, vac)
    y_a_ctx = y_a_ctx.reshape(y_a_ctx.shape[:2] + (BRANCH_W,))
    y_c_ctx = diff_out(diff_attend(qdc, kdc, vdc, lam), diff_subln_g, lam_init)
    y_b_ctx = conv_module(buc, conv_w, conv_b, conv_ln_g, conv_ln_b)
    y_d_ctx = fourier_mix(duc)
    y_ctx = merge(y_a_ctx, y_b_ctx, y_c_ctx, y_d_ctx, glc, w_branch, w_out)
    return y, y_ctx


def setup_inputs(seed: int = 0) -> dict:
    key = jax.random.key(seed)
    ks = jax.random.split(key, 21)

    def nrm(k, shape, scale):
        return jax.random.normal(k, shape, jnp.float32) * scale

    return {
        'x': nrm(ks[0], (BATCH, SEQ, D_MODEL), 1.0),
        'c': nrm(ks[1], (BATCH, D_MODEL), 1.0),
        'ctx': nrm(ks[2], (BATCH, CTX_LEN, D_MODEL), 1.0),
        'c_ctx': nrm(ks[3], (D_MODEL,), 1.0),
        'ada_w': nrm(ks[4], (DEPTH, D_MODEL, N_MOD * D_MODEL), 0.5 * D_MODEL ** -0.5),
        'ada_b': nrm(ks[5], (DEPTH, N_MOD * D_MODEL), 0.01),
        'norm_g': 1.0 + nrm(ks[6], (DEPTH, 3, D_MODEL), 0.02),
        'ffn_w1': nrm(ks[7], (DEPTH, 2, D_MODEL, D_FF), D_MODEL ** -0.5),
        'ffn_w3': nrm(ks[8], (DEPTH, 2, D_MODEL, D_FF), D_MODEL ** -0.5),
        'ffn_w2': nrm(ks[9], (DEPTH, 2, D_FF, D_MODEL), D_FF ** -0.5),
        'w_in': nrm(ks[10], (DEPTH, D_MODEL, IN_COLS), D_MODEL ** -0.5),
        'qk_norm_a': 1.0 + nrm(ks[11], (DEPTH, 2, HEAD_DIM), 0.02),
        'conv_w': nrm(ks[12], (DEPTH, CONV_W, B_CH), CONV_W ** -0.5),
        'conv_b': nrm(ks[13], (DEPTH, B_CH), 0.01),
        'conv_ln_g': 1.0 + nrm(ks[14], (DEPTH, B_CH), 0.02),
        'conv_ln_b': nrm(ks[15], (DEPTH, B_CH), 0.01),
        'diff_lam': nrm(ks[16], (DEPTH, 4, HEAD_DIM), 0.1),
        'diff_subln_g': 1.0 + nrm(ks[17], (DEPTH, C_V_DIM), 0.02),
        'w_branch': nrm(ks[18], (DEPTH, N_BRANCH, BRANCH_W, D_MODEL), BRANCH_W ** -0.5),
        'w_out': nrm(ks[19], (DEPTH, D_MODEL, D_MODEL), D_MODEL ** -0.5),
        'final_g': 1.0 + nrm(ks[20], (D_MODEL,), 0.02),
    }


def reference(x, c, ctx, c_ctx, ada_w, ada_b, norm_g, ffn_w1, ffn_w3, ffn_w2, w_in, qk_norm_a,
              conv_w, conv_b, conv_ln_g, conv_ln_b, diff_lam, diff_subln_g, w_branch, w_out, final_g):
    cos, sin = axial_rope_tables(x.shape[1])
    h, hc = x, ctx
    for l in range(DEPTH):
        update_ctx = l < DEPTH - 1
        lam_init = 0.8 - 0.6 * math.exp(-0.3 * l)
        m = adaln(c, ada_w[l], ada_b[l])
        mc = adaln(c_ctx[None, :], ada_w[l], ada_b[l])
        h = half_ffn(h, m[:, 0], m[:, 1], m[:, 2], norm_g[l, 0], ffn_w1[l, 0], ffn_w3[l, 0], ffn_w2[l, 0])
        hc = half_ffn(hc, mc[:, 0], mc[:, 1], mc[:, 2], norm_g[l, 0], ffn_w1[l, 0], ffn_w3[l, 0], ffn_w2[l, 0])
        n = modulate(rms_norm(h, norm_g[l, 1]), m[:, 3], m[:, 4])
        nc = modulate(rms_norm(hc, norm_g[l, 1]), mc[:, 3], mc[:, 4])
        y, y_ctx = token_mixing(n, nc, w_in[l], qk_norm_a[l], conv_w[l], conv_b[l], conv_ln_g[l],
                                conv_ln_b[l], diff_lam[l], diff_subln_g[l], w_branch[l], w_out[l],
                                lam_init, cos, sin, update_ctx)
        h = h + m[:, 5] * y
        h = half_ffn(h, m[:, 6], m[:, 7], m[:, 8], norm_g[l, 2], ffn_w1[l, 1], ffn_w3[l, 1], ffn_w2[l, 1])
        if update_ctx:
            hc = hc + mc[:, 5] * y_ctx
            hc = half_ffn(hc, mc[:, 6], mc[:, 7], mc[:, 8], norm_g[l, 2], ffn_w1[l, 1], ffn_w3[l, 1], ffn_w2[l, 1])
    return rms_norm(h, final_g)
```

```python
import functools
import math

import jax
import jax.numpy as jnp
from jax import lax
from jax.experimental import pallas as pl
from jax.experimental.pallas import tpu as pltpu

F32 = jnp.float32
BF16 = jnp.bfloat16

D_MODEL = 1024
GRID_W = 64
HEAD_DIM = 64
ROPE_BASE = 10000.0
EPS = 1e-6
N_MOD = 9
N_BRANCH = 4
BRANCH_W = D_MODEL // 2
A_Q_HEADS = BRANCH_W // HEAD_DIM
A_KV_HEADS = 2
A_GROUP = A_Q_HEADS // A_KV_HEADS
CONV_W = 31
C_HEADS = BRANCH_W // (2 * HEAD_DIM)
D_GROUPS = 4
D_GROUP_CH = BRANCH_W // D_GROUPS
D_FF = 256 * ((8 * D_MODEL // 3 + 255) // 256)
ATTN_SCALE = HEAD_DIM ** -0.5

LANES = 128
COND_ROWS = 16
TOKEN_TILE = 512
FF_CHUNK = 256
CONV_TILE = 256
CONV_HALO = 16
CONV_ROWS = 32
ATTN_Q_ROWS = 1024
ATTN_KV_CHUNK = 512
DFT_ROW_TILE = 512
VMEM_LIMIT = 56 * 1024 * 1024


def _cparams(sem):
    return pltpu.CompilerParams(dimension_semantics=sem, vmem_limit_bytes=VMEM_LIMIT)


def _sigmoid(x):
    return 1.0 / (1.0 + jnp.exp(-x))


def _const_spec(shape):
    nd = len(shape)
    return pl.BlockSpec(shape, lambda *_: (0,) * nd, pipeline_mode=pl.Buffered(1))


def _ada_kernel(cond_ref, w_ref, b_ref, o_ref):
    cnd = cond_ref[...]
    s = cnd * _sigmoid(cnd)
    o_ref[...] = jnp.dot(s, w_ref[...], preferred_element_type=F32,
                         precision=lax.Precision.HIGHEST) + b_ref[...]


def _adaln(cond, ada_w, ada_b):
    depth = ada_w.shape[0]
    n_out = N_MOD * D_MODEL
    col = D_MODEL
    return pl.pallas_call(
        _ada_kernel,
        grid=(depth, n_out // col),
        in_specs=[
            pl.BlockSpec((COND_ROWS, D_MODEL), lambda l, j: (0, 0)),
            pl.BlockSpec((None, D_MODEL, col), lambda l, j: (l, 0, j)),
            pl.BlockSpec((None, 1, col), lambda l, j: (l, 0, j)),
        ],
        out_specs=pl.BlockSpec((None, COND_ROWS, col), lambda l, j: (l, 0, j)),
        out_shape=jax.ShapeDtypeStruct((depth, COND_ROWS, n_out), F32),
        compiler_params=_cparams(("arbitrary", "arbitrary")),
        name="adaln",
    )(cond, ada_w, ada_b.reshape(depth, 1, n_out))


def _rms(x, g):
    return x * lax.rsqrt(jnp.mean(x * x, axis=-1, keepdims=True) + EPS) * g


def _mod_spec(layer, sub, lat_tiles, tiles_per_batch, ctx_row):
    def idx(i):
        row = jnp.where(i < lat_tiles, i // tiles_per_batch, ctx_row)
        return (layer, row, sub, 0, 0)
    return pl.BlockSpec((None, None, None, 3, D_MODEL), idx)


def _ffn_kernel(h_ref, mod_ref, g_ref, w13_ref, w2_ref, *rest, final):
    if final:
        fg_ref, o_ref = rest
    else:
        (o_ref,) = rest
    x = h_ref[...]
    shift, scale, gate = mod_ref[0:1, :], mod_ref[1:2, :], mod_ref[2:3, :]
    n = (_rms(x, g_ref[...]) * (1.0 + scale) + shift).astype(BF16)
    acc = jnp.zeros(x.shape, F32)
    c = FF_CHUNK
    for j in range(D_FF // c):
        ab = jnp.dot(n, w13_ref[:, 2 * c * j:2 * c * (j + 1)], preferred_element_type=F32)
        a, b = ab[:, :c], ab[:, c:]
        p = (a * _sigmoid(a) * b).astype(BF16)
        acc = acc + jnp.dot(p, w2_ref[c * j:c * (j + 1), :], preferred_element_type=F32)
    out = x + 0.5 * gate * acc
    if final:
        out = _rms(out, fg_ref[...])
    o_ref[...] = out


def _half_ffn(h, mods, layer, sub, g, w13, w2, n_tiles, lat_tiles, tiles_per_batch, ctx_row,
              final_g=None):
    tm = TOKEN_TILE
    final = final_g is not None
    in_specs = [
        pl.BlockSpec((tm, D_MODEL), lambda i: (i, 0)),
        _mod_spec(layer, sub, lat_tiles, tiles_per_batch, ctx_row),
        _const_spec((1, D_MODEL)),
        _const_spec((D_MODEL, 2 * D_FF)),
        _const_spec((D_FF, D_MODEL)),
    ]
    args = [h, mods, g.reshape(1, D_MODEL), w13, w2]
    if final:
        in_specs.append(_const_spec((1, D_MODEL)))
        args.append(final_g.reshape(1, D_MODEL))
    return pl.pallas_call(
        functools.partial(_ffn_kernel, final=final),
        grid=(n_tiles,),
        in_specs=in_specs,
        out_specs=pl.BlockSpec((tm, D_MODEL), lambda i: (i, 0)),
        out_shape=jax.ShapeDtypeStruct((n_tiles * tm, D_MODEL), F32),
        compiler_params=_cparams(("parallel",)),
        name=f"ffn_l{layer}_s{sub}",
    )(*args)


_IN_SEGS = (("aq", 512), ("ak", 128), ("av", 128), ("ba", 512), ("bg", 512),
            ("cq", 512), ("ck", 512), ("cv", 512), ("du", 512), ("gl", 4096))
_IN_OFF = {}
_o = 0
for _n, _s in _IN_SEGS:
    _IN_OFF[_n] = (_o, _o + _s)
    _o += _s
IN_COLS = _o


def _rope_blocks(x, cos, sin_s, low16):
    outs = []
    for j in range(x.shape[1] // LANES):
        xb = x[:, j * LANES:(j + 1) * LANES]
        partner = jnp.where(low16, pltpu.roll(xb, LANES - 16, 1), pltpu.roll(xb, 16, 1))
        outs.append(xb * cos + partner * sin_s)
    return outs


def _in_kernel(h_ref, mod_ref, g_ref, w_ref, cos_ref, sin_ref, seg_ref, qkg_ref,
               qa_ref, ka_ref, va_ref, yg_ref, qc_ref, kc_ref, vc_ref, du_ref, gt_ref):
    x = h_ref[...]
    shift, scale = mod_ref[0:1, :], mod_ref[1:2, :]
    n = (_rms(x, g_ref[...]) * (1.0 + scale) + shift).astype(BF16)

    def proj(name, lo=None, hi=None):
        a, b = _IN_OFF[name]
        if lo is not None:
            a, b = a + lo, a + hi
        return jnp.dot(n, w_ref[:, a:b], preferred_element_type=F32)

    cos, sin_s = cos_ref[...], sin_ref[...]
    lane = lax.broadcasted_iota(jnp.int32, (1, LANES), 1)
    low16 = (lane & 16) == 0
    seg = seg_ref[...]

    def head_norm(z, g):
        outs = []
        for j in range(z.shape[1] // LANES):
            zb = z[:, j * LANES:(j + 1) * LANES]
            ss = jnp.dot((zb * zb).astype(BF16), seg, preferred_element_type=F32)
            outs.append(zb * lax.rsqrt(ss * (1.0 / HEAD_DIM) + EPS) * g)
        return jnp.concatenate(outs, axis=1)

    q = head_norm(proj("aq"), qkg_ref[0:1, :])
    qa_ref[...] = jnp.concatenate(
        [b * ATTN_SCALE for b in _rope_blocks(q, cos, sin_s, low16)], axis=1).astype(BF16)
    k = head_norm(proj("ak"), qkg_ref[1:2, :])
    ka_ref[...] = _rope_blocks(k, cos, sin_s, low16)[0].astype(BF16)
    va_ref[...] = proj("av").astype(BF16)
    yg_ref[...] = proj("ba") * _sigmoid(proj("bg"))
    qc_ref[...] = jnp.concatenate(
        [b * ATTN_SCALE for b in _rope_blocks(proj("cq"), cos, sin_s, low16)], axis=1).astype(BF16)
    kc_ref[...] = jnp.concatenate(_rope_blocks(proj("ck"), cos, sin_s, low16), axis=1).astype(BF16)
    vc_ref[...] = proj("cv").astype(BF16)
    du_ref[...] = proj("du").astype(BF16)
    for j in range(N_BRANCH):
        gt_ref[:, j * D_MODEL:(j + 1) * D_MODEL] = _sigmoid(
            proj("gl", j * D_MODEL, (j + 1) * D_MODEL)).astype(BF16)


def _in_proj(h, mods, layer, g, w, cos_t, sin_t, seg, qkg, n_tiles, lat_tiles, tiles_per_batch,
             ctx_row):
    tm = TOKEN_TILE
    t = n_tiles * tm

    def rope_idx(i):
        return (jnp.where(i < lat_tiles, i % tiles_per_batch, tiles_per_batch), 0)

    widths = (512, 128, 128, 512, 512, 512, 512, 512, 4096)
    dtypes = (BF16, BF16, BF16, F32, BF16, BF16, BF16, BF16, BF16)
    return pl.pallas_call(
        _in_kernel,
        grid=(n_tiles,),
        in_specs=[
            pl.BlockSpec((tm, D_MODEL), lambda i: (i, 0)),
            _mod_spec(layer, 1, lat_tiles, tiles_per_batch, ctx_row),
            _const_spec((1, D_MODEL)),
            _const_spec((D_MODEL, IN_COLS)),
            pl.BlockSpec((tm, LANES), rope_idx),
            pl.BlockSpec((tm, LANES), rope_idx),
            _const_spec((LANES, LANES)),
            _const_spec((2, LANES)),
        ],
        out_specs=[pl.BlockSpec((tm, wd), lambda i: (i, 0)) for wd in widths],
        out_shape=[jax.ShapeDtypeStruct((t, wd), dt) for wd, dt in zip(widths, dtypes)],
        compiler_params=_cparams(("parallel",)),
        name=f"in_proj_l{layer}",
    )(h, mods, g.reshape(1, D_MODEL), w, cos_t, sin_t, seg, qkg)


def _flash(qp, kv_sources, col, m_ref, l_ref, acc_ref):
    m_ref[...] = jnp.full(m_ref.shape, -jnp.inf, F32)
    l_ref[...] = jnp.zeros(l_ref.shape, F32)
    acc_ref[...] = jnp.zeros(acc_ref.shape, F32)
    c0 = col * LANES

    def step(k, v):
        s = lax.dot_general(qp, k, (((1,), (1,)), ((), ())), preferred_element_type=F32)
        m_old = m_ref[...]
        m_new = jnp.maximum(m_old, jnp.max(s, axis=-1, keepdims=True))
        alpha = jnp.exp(m_old - m_new)
        p = jnp.exp(s - m_new)
        l_ref[...] = alpha * l_ref[...] + jnp.sum(p, axis=-1, keepdims=True)
        acc_ref[...] = alpha * acc_ref[...] + jnp.dot(p.astype(BF16), v,
                                                      preferred_element_type=F32)
        m_ref[...] = m_new

    for k_ref, v_ref, n_keys in kv_sources:
        tk = min(ATTN_KV_CHUNK, n_keys)
        n_chunks = n_keys // tk
        if n_chunks == 1:
            step(k_ref[:, c0:c0 + LANES], v_ref[:, c0:c0 + LANES])
        else:
            def body(c, carry, k_ref=k_ref, v_ref=v_ref, tk=tk):
                r0 = pl.multiple_of(c * tk, tk)
                step(k_ref[pl.ds(r0, tk), c0:c0 + LANES], v_ref[pl.ds(r0, tk), c0:c0 + LANES])
                return carry
            lax.fori_loop(0, n_chunks, body, 0)


def _half_masks():
    lane = lax.broadcasted_iota(jnp.int32, (1, LANES), 1)
    return lane < HEAD_DIM


def _gqa_kernel(*refs, sources, tq):
    q_ref = refs[0]
    kv_refs = refs[1:1 + 2 * len(sources)]
    o_ref = refs[1 + 2 * len(sources)]
    m_ref, l_ref, acc_ref = refs[2 + 2 * len(sources):]
    low = _half_masks()
    q = q_ref[...].astype(F32)
    nb = A_GROUP
    blocks = [q[:, j * LANES:(j + 1) * LANES] for j in range(nb)]
    qp = jnp.concatenate([jnp.where(low, b, 0.0) for b in blocks]
                         + [jnp.where(low, 0.0, b) for b in blocks], axis=0).astype(BF16)
    kv = [(kv_refs[2 * s], kv_refs[2 * s + 1], sources[s]) for s in range(len(sources))]
    _flash(qp, kv, 0, m_ref, l_ref, acc_ref)
    o = acc_ref[...] / l_ref[...]
    for j in range(nb):
        o_ref[:, j * LANES:(j + 1) * LANES] = jnp.where(
            low, o[j * tq:(j + 1) * tq, :], o[(nb + j) * tq:(nb + j + 1) * tq, :]).astype(BF16)


def _diff_kernel(*refs, sources, tq, lam_init):
    q_ref = refs[0]
    kv_refs = refs[1:1 + 2 * len(sources)]
    lam_ref, g_ref, o_ref = refs[1 + 2 * len(sources):4 + 2 * len(sources)]
    m_ref, l_ref, acc_ref = refs[4 + 2 * len(sources):]
    low = _half_masks()
    lp = lam_ref[...]
    lam = (jnp.exp(jnp.sum(lp[0:1, :] * lp[1:2, :], axis=-1, keepdims=True))
           - jnp.exp(jnp.sum(lp[2:3, :] * lp[3:4, :], axis=-1, keepdims=True)) + lam_init)
    kv = [(kv_refs[2 * s], kv_refs[2 * s + 1], sources[s]) for s in range(len(sources))]
    for h in range(C_HEADS):
        qb = q_ref[:, h * LANES:(h + 1) * LANES].astype(F32)
        qp = jnp.concatenate([jnp.where(low, qb, 0.0), jnp.where(low, 0.0, qb)],
                             axis=0).astype(BF16)
        _flash(qp, kv, h, m_ref, l_ref, acc_ref)
        o = acc_ref[...] / l_ref[...]
        d = o[0:tq, :] - lam * o[tq:2 * tq, :]
        o_ref[:, h * LANES:(h + 1) * LANES] = (
            _rms(d, g_ref[...]) * (1.0 - lam_init)).astype(BF16)


def _attention(kind, q, k, v, extra, n_batch, q_rows, q_row0, sources, lam_init=None, name=""):
    stack = 2 * A_GROUP if kind == "gqa" else 2
    tq = min(ATTN_Q_ROWS // stack, q_rows)
    n_q = q_rows // tq
    in_specs = [pl.BlockSpec((tq, BRANCH_W), lambda b, i: (q_row0 // tq + b * n_q + i, 0))]
    args = [q]
    for row0, n_keys in sources:
        for arr in (k, v):
            in_specs.append(pl.BlockSpec(
                (n_keys, arr.shape[1]),
                functools.partial(lambda b, i, r: (r + b, 0), r=row0 // n_keys)))
            args.append(arr)
    n_keys_list = tuple(nk for _, nk in sources)
    if kind == "gqa":
        body = functools.partial(_gqa_kernel, sources=n_keys_list, tq=tq)
    else:
        body = functools.partial(_diff_kernel, sources=n_keys_list, tq=tq, lam_init=lam_init)
        in_specs += [pl.BlockSpec(extra[0].shape, lambda b, i: (0, 0)),
                     pl.BlockSpec(extra[1].shape, lambda b, i: (0, 0))]
        args += list(extra)
    m_rows = stack * tq
    return pl.pallas_call(
        body,
        grid=(n_batch, n_q),
        in_specs=in_specs,
        out_specs=pl.BlockSpec((tq, BRANCH_W), lambda b, i: (b * n_q + i, 0)),
        out_shape=jax.ShapeDtypeStruct((n_batch * q_rows, BRANCH_W), BF16),
        scratch_shapes=[pltpu.VMEM((m_rows, 1), F32), pltpu.VMEM((m_rows, 1), F32),
                        pltpu.VMEM((m_rows, LANES), F32)],
        compiler_params=_cparams(("parallel", "parallel")),
        name=name,
    )(*args)


def _conv_kernel(prev_ref, cur_ref, next_ref, w_ref, b_ref, g_ref, beta_ref, o_ref, win_ref,
                 *, lat_tiles, lat_per_seq, ctx_per_seq):
    i = pl.program_id(0)
    j = jnp.where(i < lat_tiles, i % lat_per_seq, (i - lat_tiles) % ctx_per_seq)
    per = jnp.where(i < lat_tiles, lat_per_seq, ctx_per_seq)
    first, last = j == 0, j == per - 1
    hl, tm = CONV_HALO, CONV_TILE
    win_ref[0:hl, :] = jnp.where(first, 0.0, prev_ref[...])
    win_ref[hl:hl + tm, :] = cur_ref[...]
    win_ref[hl + tm:hl + tm + hl, :] = jnp.where(last, 0.0, next_ref[...])
    half = CONV_W // 2
    rc = CONV_ROWS
    for c in range(tm // rc):
        acc = jnp.zeros((rc, BRANCH_W), F32)
        for k in range(CONV_W):
            r0 = c * rc + k + hl - half
            acc = acc + win_ref[r0:r0 + rc, :] * w_ref[k:k + 1, :]
        y = acc + b_ref[...]
        mu = jnp.mean(y, axis=-1, keepdims=True)
        yc = y - mu
        var = jnp.mean(yc * yc, axis=-1, keepdims=True)
        yn = yc * lax.rsqrt(var + EPS) * g_ref[...] + beta_ref[...]
        o_ref[c * rc:(c + 1) * rc, :] = (yn * _sigmoid(yn)).astype(BF16)


def _conv_module(y, w, b, g, beta, n_rows, lat_rows, seq, ctx_len):
    tm, hl = CONV_TILE, CONV_HALO
    n_tiles = n_rows // tm
    r = tm // hl
    n_halo = y.shape[0] // hl
    body = functools.partial(_conv_kernel, lat_tiles=lat_rows // tm, lat_per_seq=seq // tm,
                             ctx_per_seq=max(ctx_len // tm, 1))
    vec = lambda a: a.reshape(1, BRANCH_W)
    return pl.pallas_call(
        body,
        grid=(n_tiles,),
        in_specs=[
            pl.BlockSpec((hl, BRANCH_W), lambda i: (jnp.maximum(i * r - 1, 0), 0)),
            pl.BlockSpec((tm, BRANCH_W), lambda i: (i, 0)),
            pl.BlockSpec((hl, BRANCH_W), lambda i: (jnp.minimum((i + 1) * r, n_halo - 1), 0)),
            _const_spec((CONV_W, BRANCH_W)),
            _const_spec((1, BRANCH_W)), _const_spec((1, BRANCH_W)), _const_spec((1, BRANCH_W)),
        ],
        out_specs=pl.BlockSpec((tm, BRANCH_W), lambda i: (i, 0)),
        out_shape=jax.ShapeDtypeStruct((n_rows, BRANCH_W), BF16),
        scratch_shapes=[pltpu.VMEM((tm + 2 * hl, BRANCH_W), F32)],
        compiler_params=_cparams(("parallel",)),
        name="conv_module",
    )(y, y, y, w, vec(b), vec(g), vec(beta))


def _dft_kernel(u_ref, cc_ref, sc_ref, t_ref, o_ref, z_ref, *, seq):
    @pl.when(pl.program_id(1) == 0)
    def _():
        for gidx in range(D_GROUPS):
            ub = u_ref[:, gidx * LANES:(gidx + 1) * LANES]
            z_ref[0:seq, gidx * LANES:(gidx + 1) * LANES] = jnp.dot(
                ub, cc_ref[...], preferred_element_type=F32).astype(BF16)
            z_ref[seq:2 * seq, gidx * LANES:(gidx + 1) * LANES] = jnp.dot(
                ub, sc_ref[...], preferred_element_type=F32).astype(BF16)
    o_ref[...] = jnp.dot(t_ref[...], z_ref[...], preferred_element_type=F32).astype(BF16)


def _dft_tables(seq):
    def cs(n):
        idx = jnp.arange(n, dtype=jnp.int32)
        ang = ((idx[:, None] * idx[None, :]) % n).astype(F32) * (2.0 * math.pi / n)
        return jnp.cos(ang), jnp.sin(ang)
    ct, st = cs(seq)
    cc, sc = cs(D_GROUP_CH)
    norm = 1.0 / math.sqrt(seq * D_GROUP_CH)
    tmat = jnp.concatenate([ct, -st], axis=1).astype(BF16)
    return (cc * norm).astype(BF16), (sc * norm).astype(BF16), tmat


def _fourier_mix(u, tables, n_batch, seq, row0, name):
    cc, sc, tmat = tables
    tr = min(DFT_ROW_TILE, seq)
    n_r = seq // tr
    return pl.pallas_call(
        functools.partial(_dft_kernel, seq=seq),
        grid=(n_batch, n_r),
        in_specs=[
            pl.BlockSpec((seq, BRANCH_W), lambda b, r: (row0 // seq + b, 0)),
            pl.BlockSpec((LANES, LANES), lambda b, r: (0, 0)),
            pl.BlockSpec((LANES, LANES), lambda b, r: (0, 0)),
            pl.BlockSpec((tr, 2 * seq), lambda b, r: (r, 0)),
        ],
        out_specs=pl.BlockSpec((tr, BRANCH_W), lambda b, r: (b * n_r + r, 0)),
        out_shape=jax.ShapeDtypeStruct((n_batch * seq, BRANCH_W), BF16),
        scratch_shapes=[pltpu.VMEM((2 * seq, BRANCH_W), BF16)],
        compiler_params=_cparams(("parallel", "arbitrary")),
        name=name,
    )(u, cc, sc, tmat)


def _merge_kernel(ya_ref, yb_ref, yc_ref, yd_ref, gt_ref, h_ref, mod_ref, wb_ref, wo_ref, o_ref):
    acc = None
    for j, y_ref in enumerate((ya_ref, yb_ref, yc_ref, yd_ref)):
        proj = jnp.dot(y_ref[...], wb_ref[j], preferred_element_type=F32)
        term = gt_ref[:, j * D_MODEL:(j + 1) * D_MODEL].astype(F32) * proj
        acc = term if acc is None else acc + term
    y = jnp.dot(acc.astype(BF16), wo_ref[...], preferred_element_type=F32)
    o_ref[...] = h_ref[...] + mod_ref[2:3, :] * y


def _merge(ys, gates, h, mods, layer, wb, wo, n_tiles, lat_tiles, tiles_per_batch, ctx_row):
    tm = TOKEN_TILE
    row = lambda wd: pl.BlockSpec((tm, wd), lambda i: (i, 0))
    return pl.pallas_call(
        _merge_kernel,
        grid=(n_tiles,),
        in_specs=[row(BRANCH_W)] * 4 + [
            row(N_BRANCH * D_MODEL), row(D_MODEL),
            _mod_spec(layer, 1, lat_tiles, tiles_per_batch, ctx_row),
            _const_spec((N_BRANCH, BRANCH_W, D_MODEL)),
            _const_spec((D_MODEL, D_MODEL)),
        ],
        out_specs=row(D_MODEL),
        out_shape=jax.ShapeDtypeStruct((n_tiles * tm, D_MODEL), F32),
        compiler_params=_cparams(("parallel",)),
        name=f"merge_l{layer}",
    )(*ys, gates, h, mods, wb, wo)


def _rope_tables(seq, pad_rows):
    pos = jnp.arange(seq, dtype=jnp.int32)
    row = (pos // GRID_W).astype(F32)
    col = (pos % GRID_W).astype(F32)
    half = HEAD_DIM // 2
    inv = ROPE_BASE ** (-jnp.arange(0, half, 2, dtype=F32) / half)
    ang_r, ang_c = row[:, None] * inv, col[:, None] * inv
    ang = jnp.concatenate([ang_r, ang_r, ang_c, ang_c], axis=-1)
    cos, sin = jnp.cos(ang), jnp.sin(ang)
    lane = jnp.arange(HEAD_DIM)
    sin_s = jnp.where((lane & 16) == 0, -sin, sin)
    cos = jnp.concatenate([cos, jnp.ones((pad_rows, HEAD_DIM), F32)], axis=0)
    sin_s = jnp.concatenate([sin_s, jnp.zeros((pad_rows, HEAD_DIM), F32)], axis=0)
    return jnp.tile(cos, (1, 2)), jnp.tile(sin_s, (1, 2))


def _arrange_in_weight(w_in):
    aq = w_in[:, :BRANCH_W].reshape(D_MODEL, A_KV_HEADS, A_GROUP, HEAD_DIM)
    aq = jnp.transpose(aq, (0, 2, 1, 3)).reshape(D_MODEL, BRANCH_W)
    return jnp.concatenate([aq, w_in[:, BRANCH_W:]], axis=1).astype(BF16)


def _arrange_ffn(w1, w3):
    nch = D_FF // FF_CHUNK
    w13 = jnp.stack([w1.reshape(D_MODEL, nch, FF_CHUNK), w3.reshape(D_MODEL, nch, FF_CHUNK)],
                    axis=2)
    return w13.reshape(D_MODEL, 2 * D_FF).astype(BF16)


def kernel(x, c, ctx, c_ctx, ada_w, ada_b, norm_g, ffn_w1, ffn_w3, ffn_w2, w_in, qk_norm_a,
           conv_w, conv_b, conv_ln_g, conv_ln_b, diff_lam, diff_subln_g, w_branch, w_out, final_g):
    n_batch, seq, _ = x.shape
    ctx_len = ctx.shape[1]
    depth = ada_w.shape[0]
    tm = TOKEN_TILE
    lat_rows, ctx_rows = n_batch * seq, n_batch * ctx_len
    assert seq % tm == 0 and ctx_rows % tm == 0 and ctx_len % CONV_TILE == 0
    assert n_batch < COND_ROWS and seq % GRID_W == 0
    lat_tiles, all_tiles = lat_rows // tm, (lat_rows + ctx_rows) // tm
    tiles_per_batch = seq // tm
    ctx_row = n_batch

    cond = jnp.concatenate(
        [c, c_ctx[None, :], jnp.zeros((COND_ROWS - n_batch - 1, D_MODEL), F32)], axis=0)
    mods = _adaln(cond, ada_w, ada_b).reshape(depth, COND_ROWS, 3, 3, D_MODEL)

    cos_t, sin_t = _rope_tables(seq, tm)
    lane = jnp.arange(LANES)
    seg = (lane[:, None] // HEAD_DIM == lane[None, :] // HEAD_DIM).astype(BF16)
    dft_lat = _dft_tables(seq)
    dft_ctx = _dft_tables(ctx_len)
    tile_args = (lat_tiles, tiles_per_batch, ctx_row)

    h = jnp.concatenate([x.reshape(lat_rows, D_MODEL), ctx.reshape(ctx_rows, D_MODEL)], axis=0)
    for l in range(depth):
        update_ctx = l < depth - 1
        lam_init = 0.8 - 0.6 * math.exp(-0.3 * l)
        out_tiles = all_tiles if update_ctx else lat_tiles
        out_rows = out_tiles * tm
        w13a, w13b = _arrange_ffn(ffn_w1[l, 0], ffn_w3[l, 0]), _arrange_ffn(ffn_w1[l, 1], ffn_w3[l, 1])
        w2a, w2b = ffn_w2[l, 0].astype(BF16), ffn_w2[l, 1].astype(BF16)
        wb_a = w_branch[l, 0].reshape(A_KV_HEADS, A_GROUP, HEAD_DIM, D_MODEL)
        wb_a = jnp.transpose(wb_a, (1, 0, 2, 3)).reshape(BRANCH_W, D_MODEL)
        wb = jnp.concatenate([wb_a[None], w_branch[l, 1:]], axis=0).astype(BF16)
        qkg = jnp.tile(qk_norm_a[l], (1, 2))

        h = _half_ffn(h, mods, l, 0, norm_g[l, 0], w13a, w2a, all_tiles, *tile_args)
        qa, ka, va, yg, qc, kc, vc, du, gates = _in_proj(
            h, mods, l, norm_g[l, 1], _arrange_in_weight(w_in[l]), cos_t, sin_t, seg, qkg,
            all_tiles, *tile_args)

        lat_src = ((0, seq), (lat_rows, ctx_len))
        ctx_src = ((lat_rows, ctx_len),)
        diff_extra = (diff_lam[l], diff_subln_g[l].reshape(1, LANES))
        y_a = _attention("gqa", qa, ka, va, None, n_batch, seq, 0, lat_src, name=f"gqa_l{l}")
        y_c = _attention("diff", qc, kc, vc, diff_extra, n_batch, seq, 0, lat_src,
                         lam_init=lam_init, name=f"diff_l{l}")
        y_b = _conv_module(yg, conv_w[l], conv_b[l], conv_ln_g[l], conv_ln_b[l],
                           out_rows, lat_rows, seq, ctx_len)
        y_d = _fourier_mix(du, dft_lat, n_batch, seq, 0, f"dft_l{l}")
        if update_ctx:
            y_a = jnp.concatenate([y_a, _attention(
                "gqa", qa, ka, va, None, n_batch, ctx_len, lat_rows, ctx_src,
                name=f"gqa_ctx_l{l}")], axis=0)
            y_c = jnp.concatenate([y_c, _attention(
                "diff", qc, kc, vc, diff_extra, n_batch, ctx_len, lat_rows, ctx_src,
                lam_init=lam_init, name=f"diff_ctx_l{l}")], axis=0)
            y_d = jnp.concatenate(
                [y_d, _fourier_mix(du, dft_ctx, n_batch, ctx_len, lat_rows, f"dft_ctx_l{l}")],
                axis=0)
        h = _merge((y_a, y_b, y_c, y_d), gates, h, mods, l, wb, w_out[l].astype(BF16),
                   out_tiles, *tile_args)
        h = _half_ffn(h, mods, l, 2, norm_g[l, 2], w13b, w2b, out_tiles, *tile_args,
                      final_g=None if update_ctx else final_g)
    return h.reshape(n_batch, seq, D_MODEL)
```

```python
import functools
import math

import jax
import jax.numpy as jnp
from jax import lax
from jax.experimental import pallas as pl
from jax.experimental.pallas import tpu as pltpu

F32 = jnp.float32
BF16 = jnp.bfloat16

D_MODEL = 1024
GRID_W = 64
HEAD_DIM = 64
ROPE_BASE = 10000.0
EPS = 1e-6
N_MOD = 9
N_BRANCH = 4
BRANCH_W = D_MODEL // 2
A_Q_HEADS = BRANCH_W // HEAD_DIM
A_KV_HEADS = 2
A_GROUP = A_Q_HEADS // A_KV_HEADS
CONV_W = 31
C_HEADS = BRANCH_W // (2 * HEAD_DIM)
D_GROUPS = 4
D_GROUP_CH = BRANCH_W // D_GROUPS
D_FF = 256 * ((8 * D_MODEL // 3 + 255) // 256)
ATTN_SCALE = HEAD_DIM ** -0.5

LANES = 128
COND_ROWS = 16
TOKEN_TILE = 512
FF_CHUNK = 256
CONV_TILE = 256
CONV_HALO = 16
CONV_ROWS = 32
ATTN_Q_ROWS = 512
Q_SCALE = ATTN_SCALE * math.log2(math.e)
DFT_ROW_TILE = 512
VMEM_LIMIT = 56 * 1024 * 1024


def _cparams(sem):
    return pltpu.CompilerParams(dimension_semantics=sem, vmem_limit_bytes=VMEM_LIMIT)


def _sigmoid(x):
    return 1.0 / (1.0 + jnp.exp(-x))


def _const_spec(shape):
    nd = len(shape)
    return pl.BlockSpec(shape, lambda *_: (0,) * nd, pipeline_mode=pl.Buffered(1))


def _ada_kernel(cond_ref, w_ref, b_ref, o_ref):
    cnd = cond_ref[...]
    s = cnd * _sigmoid(cnd)
    o_ref[...] = jnp.dot(s, w_ref[...], preferred_element_type=F32,
                         precision=lax.Precision.HIGHEST) + b_ref[...]


def _adaln(cond, ada_w, ada_b):
    depth = ada_w.shape[0]
    n_out = N_MOD * D_MODEL
    col = D_MODEL
    return pl.pallas_call(
        _ada_kernel,
        grid=(depth, n_out // col),
        in_specs=[
            pl.BlockSpec((COND_ROWS, D_MODEL), lambda l, j: (0, 0)),
            pl.BlockSpec((None, D_MODEL, col), lambda l, j: (l, 0, j)),
            pl.BlockSpec((None, 1, col), lambda l, j: (l, 0, j)),
        ],
        out_specs=pl.BlockSpec((None, COND_ROWS, col), lambda l, j: (l, 0, j)),
        out_shape=jax.ShapeDtypeStruct((depth, COND_ROWS, n_out), F32),
        compiler_params=_cparams(("arbitrary", "arbitrary")),
        name="adaln",
    )(cond, ada_w, ada_b.reshape(depth, 1, n_out))


def _rms(x, g):
    return x * lax.rsqrt(jnp.mean(x * x, axis=-1, keepdims=True) + EPS) * g


def _mod_spec(layer, sub, lat_tiles, tiles_per_batch, ctx_row):
    def idx(i):
        row = jnp.where(i < lat_tiles, i // tiles_per_batch, ctx_row)
        return (layer, row, sub, 0, 0)
    return pl.BlockSpec((None, None, None, 3, D_MODEL), idx)


def _ffn_kernel(h_ref, mod_ref, g_ref, w13_ref, w2_ref, *rest, final):
    if final:
        fg_ref, o_ref = rest
    else:
        (o_ref,) = rest
    x = h_ref[...]
    shift, scale, gate = mod_ref[0:1, :], mod_ref[1:2, :], mod_ref[2:3, :]
    n = (_rms(x, g_ref[...]) * (1.0 + scale) + shift).astype(BF16)
    acc = jnp.zeros(x.shape, F32)
    c = FF_CHUNK
    for j in range(D_FF // c):
        ab = jnp.dot(n, w13_ref[:, 2 * c * j:2 * c * (j + 1)], preferred_element_type=F32)
        a, b = ab[:, :c], ab[:, c:]
        p = (a * _sigmoid(a) * b).astype(BF16)
        acc = acc + jnp.dot(p, w2_ref[c * j:c * (j + 1), :], preferred_element_type=F32)
    out = x + 0.5 * gate * acc
    if final:
        out = _rms(out, fg_ref[...])
    o_ref[...] = out


def _half_ffn(h, mods, layer, sub, g, w13, w2, n_tiles, lat_tiles, tiles_per_batch, ctx_row,
              final_g=None):
    tm = TOKEN_TILE
    final = final_g is not None
    in_specs = [
        pl.BlockSpec((tm, D_MODEL), lambda i: (i, 0)),
        _mod_spec(layer, sub, lat_tiles, tiles_per_batch, ctx_row),
        _const_spec((1, D_MODEL)),
        _const_spec((D_MODEL, 2 * D_FF)),
        _const_spec((D_FF, D_MODEL)),
    ]
    args = [h, mods, g.reshape(1, D_MODEL), w13, w2]
    if final:
        in_specs.append(_const_spec((1, D_MODEL)))
        args.append(final_g.reshape(1, D_MODEL))
    return pl.pallas_call(
        functools.partial(_ffn_kernel, final=final),
        grid=(n_tiles,),
        in_specs=in_specs,
        out_specs=pl.BlockSpec((tm, D_MODEL), lambda i: (i, 0)),
        out_shape=jax.ShapeDtypeStruct((n_tiles * tm, D_MODEL), F32),
        compiler_params=_cparams(("parallel",)),
        name=f"ffn_l{layer}_s{sub}",
    )(*args)


_IN_SEGS = (("aq", 512), ("ak", 128), ("av", 128), ("ba", 512), ("bg", 512),
            ("cq", 512), ("ck", 512), ("cv", 512), ("du", 512), ("gl", 4096))
_IN_OFF = {}
_o = 0
for _n, _s in _IN_SEGS:
    _IN_OFF[_n] = (_o, _o + _s)
    _o += _s
IN_COLS = _o


def _rope_blocks(x, cos, sin_s, low16):
    outs = []
    for j in range(x.shape[1] // LANES):
        xb = x[:, j * LANES:(j + 1) * LANES]
        partner = jnp.where(low16, pltpu.roll(xb, LANES - 16, 1), pltpu.roll(xb, 16, 1))
        outs.append(xb * cos + partner * sin_s)
    return outs


def _in_kernel(h_ref, mod_ref, g_ref, w_ref, cos_ref, sin_ref, seg_ref, qkg_ref,
               qa_ref, ka_ref, va_ref, yg_ref, qc_ref, kc_ref, vc_ref, du_ref, gt_ref):
    x = h_ref[...]
    shift, scale = mod_ref[0:1, :], mod_ref[1:2, :]
    n = (_rms(x, g_ref[...]) * (1.0 + scale) + shift).astype(BF16)

    def proj(name, lo=None, hi=None):
        a, b = _IN_OFF[name]
        if lo is not None:
            a, b = a + lo, a + hi
        return jnp.dot(n, w_ref[:, a:b], preferred_element_type=F32)

    cos, sin_s = cos_ref[...], sin_ref[...]
    lane = lax.broadcasted_iota(jnp.int32, (1, LANES), 1)
    low16 = (lane & 16) == 0
    seg = seg_ref[...]

    def head_norm(z, g):
        outs = []
        for j in range(z.shape[1] // LANES):
            zb = z[:, j * LANES:(j + 1) * LANES]
            ss = jnp.dot((zb * zb).astype(BF16), seg, preferred_element_type=F32)
            outs.append(zb * lax.rsqrt(ss * (1.0 / HEAD_DIM) + EPS) * g)
        return jnp.concatenate(outs, axis=1)

    q = head_norm(proj("aq"), qkg_ref[0:1, :])
    qa_ref[...] = jnp.concatenate(
        [b * Q_SCALE for b in _rope_blocks(q, cos, sin_s, low16)], axis=1).astype(BF16)
    k = head_norm(proj("ak"), qkg_ref[1:2, :])
    ka_ref[...] = _rope_blocks(k, cos, sin_s, low16)[0].astype(BF16)
    va_ref[...] = proj("av").astype(BF16)
    yg_ref[...] = proj("ba") * _sigmoid(proj("bg"))
    qc_ref[...] = jnp.concatenate(
        [b * Q_SCALE for b in _rope_blocks(proj("cq"), cos, sin_s, low16)], axis=1).astype(BF16)
    kc_ref[...] = jnp.concatenate(_rope_blocks(proj("ck"), cos, sin_s, low16), axis=1).astype(BF16)
    vc_ref[...] = proj("cv").astype(BF16)
    du_ref[...] = proj("du").astype(BF16)
    for j in range(N_BRANCH):
        gt_ref[:, j * D_MODEL:(j + 1) * D_MODEL] = _sigmoid(
            proj("gl", j * D_MODEL, (j + 1) * D_MODEL)).astype(BF16)


def _in_proj(h, mods, layer, g, w, cos_t, sin_t, seg, qkg, n_tiles, lat_tiles, tiles_per_batch,
             ctx_row):
    tm = TOKEN_TILE
    t = n_tiles * tm

    def rope_idx(i):
        return (jnp.where(i < lat_tiles, i % tiles_per_batch, tiles_per_batch), 0)

    widths = (512, 128, 128, 512, 512, 512, 512, 512, 4096)
    dtypes = (BF16, BF16, BF16, F32, BF16, BF16, BF16, BF16, BF16)
    return pl.pallas_call(
        _in_kernel,
        grid=(n_tiles,),
        in_specs=[
            pl.BlockSpec((tm, D_MODEL), lambda i: (i, 0)),
            _mod_spec(layer, 1, lat_tiles, tiles_per_batch, ctx_row),
            _const_spec((1, D_MODEL)),
            _const_spec((D_MODEL, IN_COLS)),
            pl.BlockSpec((tm, LANES), rope_idx),
            pl.BlockSpec((tm, LANES), rope_idx),
            _const_spec((LANES, LANES)),
            _const_spec((2, LANES)),
        ],
        out_specs=[pl.BlockSpec((tm, wd), lambda i: (i, 0)) for wd in widths],
        out_shape=[jax.ShapeDtypeStruct((t, wd), dt) for wd, dt in zip(widths, dtypes)],
        compiler_params=_cparams(("parallel",)),
        name=f"in_proj_l{layer}",
    )(h, mods, g.reshape(1, D_MODEL), w, cos_t, sin_t, seg, qkg)


def _attend_unit(q_ref, kv, half):
    lane_half = lax.broadcasted_iota(jnp.int32, (1, LANES), 1) // HEAD_DIM
    qp = jnp.where(lane_half == half, q_ref[...].astype(F32), 0.0).astype(BF16)
    scores = [lax.dot_general(qp, k_ref[...], (((1,), (1,)), ((), ())),
                              preferred_element_type=F32) for k_ref, _ in kv]
    mx = functools.reduce(jnp.maximum, [jnp.max(s, axis=-1, keepdims=True) for s in scores])
    acc = None
    for s, (_, v_ref) in zip(scores, kv):
        p = jnp.exp2(s - mx).astype(BF16)
        v_ext = jnp.concatenate([v_ref[...], jnp.ones(v_ref.shape, BF16)], axis=1)
        pv = jnp.dot(p, v_ext, preferred_element_type=F32)
        acc = pv if acc is None else acc + pv
    return acc[:, :LANES] / acc[:, LANES:]


def _gqa_kernel(*refs, n_src):
    q_ref, o_ref = refs[0], refs[1 + 2 * n_src]
    kv = [(refs[1 + 2 * s], refs[2 + 2 * s]) for s in range(n_src)]
    o0, o1 = _attend_unit(q_ref, kv, 0), _attend_unit(q_ref, kv, 1)
    low = lax.broadcasted_iota(jnp.int32, (1, LANES), 1) < HEAD_DIM
    o_ref[...] = jnp.where(low, o0, o1).astype(BF16)


def _diff_kernel(*refs, n_src, lam_init):
    q_ref = refs[0]
    kv = [(refs[1 + 2 * s], refs[2 + 2 * s]) for s in range(n_src)]
    lam_ref, g_ref, o_ref = refs[1 + 2 * n_src:]
    o0, o1 = _attend_unit(q_ref, kv, 0), _attend_unit(q_ref, kv, 1)
    lp = lam_ref[...]
    lam = (jnp.exp(jnp.sum(lp[0:1, :] * lp[1:2, :], axis=-1, keepdims=True))
           - jnp.exp(jnp.sum(lp[2:3, :] * lp[3:4, :], axis=-1, keepdims=True)) + lam_init)
    o_ref[...] = (_rms(o0 - lam * o1, g_ref[...]) * (1.0 - lam_init)).astype(BF16)


def _attention(kind, q, k, v, extra, n_batch, q_rows, q_row0, sources, lam_init=None, name=""):
    tq = min(ATTN_Q_ROWS, q_rows)
    n_q = q_rows // tq
    n_blocks = BRANCH_W // LANES
    kv_col = (lambda u: 0) if kind == "gqa" else (lambda u: u)
    in_specs = [pl.BlockSpec((tq, LANES), lambda b, i, u: (q_row0 // tq + b * n_q + i, u))]
    args = [q]
    for row0, n_keys in sources:
        for arr in (k, v):
            in_specs.append(pl.BlockSpec(
                (n_keys, LANES),
                functools.partial(lambda b, i, u, r: (r + b, kv_col(u)), r=row0 // n_keys)))
            args.append(arr)
    if kind == "gqa":
        body = functools.partial(_gqa_kernel, n_src=len(sources))
    else:
        body = functools.partial(_diff_kernel, n_src=len(sources), lam_init=lam_init)
        in_specs += [pl.BlockSpec(extra[0].shape, lambda b, i, u: (0, 0)),
                     pl.BlockSpec(extra[1].shape, lambda b, i, u: (0, 0))]
        args += list(extra)
    return pl.pallas_call(
        body,
        grid=(n_batch, n_q, n_blocks),
        in_specs=in_specs,
        out_specs=pl.BlockSpec((tq, LANES), lambda b, i, u: (b * n_q + i, u)),
        out_shape=jax.ShapeDtypeStruct((n_batch * q_rows, BRANCH_W), BF16),
        compiler_params=_cparams(("parallel", "parallel", "parallel")),
        name=name,
    )(*args)


def _conv_kernel(prev_ref, cur_ref, next_ref, w_ref, b_ref, g_ref, beta_ref, o_ref, win_ref,
                 *, lat_tiles, lat_per_seq, ctx_per_seq):
    i = pl.program_id(0)
    j = jnp.where(i < lat_tiles, i % lat_per_seq, (i - lat_tiles) % ctx_per_seq)
    per = jnp.where(i < lat_tiles, lat_per_seq, ctx_per_seq)
    first, last = j == 0, j == per - 1
    hl, tm = CONV_HALO, CONV_TILE
    win_ref[0:hl, :] = jnp.where(first, 0.0, prev_ref[...])
    win_ref[hl:hl + tm, :] = cur_ref[...]
    win_ref[hl + tm:hl + tm + hl, :] = jnp.where(last, 0.0, next_ref[...])
    half = CONV_W // 2
    rc = CONV_ROWS
    for c in range(tm // rc):
        acc = jnp.zeros((rc, BRANCH_W), F32)
        for k in range(CONV_W):
            r0 = c * rc + k + hl - half
            acc = acc + win_ref[r0:r0 + rc, :] * w_ref[k:k + 1, :]
        y = acc + b_ref[...]
        mu = jnp.mean(y, axis=-1, keepdims=True)
        yc = y - mu
        var = jnp.mean(yc * yc, axis=-1, keepdims=True)
        yn = yc * lax.rsqrt(var + EPS) * g_ref[...] + beta_ref[...]
        o_ref[c * rc:(c + 1) * rc, :] = (yn * _sigmoid(yn)).astype(BF16)


def _conv_module(y, w, b, g, beta, n_rows, lat_rows, seq, ctx_len):
    tm, hl = CONV_TILE, CONV_HALO
    n_tiles = n_rows // tm
    r = tm // hl
    n_halo = y.shape[0] // hl
    body = functools.partial(_conv_kernel, lat_tiles=lat_rows // tm, lat_per_seq=seq // tm,
                             ctx_per_seq=max(ctx_len // tm, 1))
    vec = lambda a: a.reshape(1, BRANCH_W)
    return pl.pallas_call(
        body,
        grid=(n_tiles,),
        in_specs=[
            pl.BlockSpec((hl, BRANCH_W), lambda i: (jnp.maximum(i * r - 1, 0), 0)),
            pl.BlockSpec((tm, BRANCH_W), lambda i: (i, 0)),
            pl.BlockSpec((hl, BRANCH_W), lambda i: (jnp.minimum((i + 1) * r, n_halo - 1), 0)),
            _const_spec((CONV_W, BRANCH_W)),
            _const_spec((1, BRANCH_W)), _const_spec((1, BRANCH_W)), _const_spec((1, BRANCH_W)),
        ],
        out_specs=pl.BlockSpec((tm, BRANCH_W), lambda i: (i, 0)),
        out_shape=jax.ShapeDtypeStruct((n_rows, BRANCH_W), BF16),
        scratch_shapes=[pltpu.VMEM((tm + 2 * hl, BRANCH_W), F32)],
        compiler_params=_cparams(("parallel",)),
        name="conv_module",
    )(y, y, y, w, vec(b), vec(g), vec(beta))


def _dft_kernel(u_ref, cc_ref, sc_ref, t_ref, o_ref, z_ref, *, seq):
    @pl.when(pl.program_id(1) == 0)
    def _():
        for gidx in range(D_GROUPS):
            ub = u_ref[:, gidx * LANES:(gidx + 1) * LANES]
            z_ref[0:seq, gidx * LANES:(gidx + 1) * LANES] = jnp.dot(
                ub, cc_ref[...], preferred_element_type=F32).astype(BF16)
            z_ref[seq:2 * seq, gidx * LANES:(gidx + 1) * LANES] = jnp.dot(
                ub, sc_ref[...], preferred_element_type=F32).astype(BF16)
    o_ref[...] = jnp.dot(t_ref[...], z_ref[...], preferred_element_type=F32).astype(BF16)


def _dft_tables(seq):
    def cs(n):
        idx = jnp.arange(n, dtype=jnp.int32)
        ang = ((idx[:, None] * idx[None, :]) % n).astype(F32) * (2.0 * math.pi / n)
        return jnp.cos(ang), jnp.sin(ang)
    ct, st = cs(seq)
    cc, sc = cs(D_GROUP_CH)
    norm = 1.0 / math.sqrt(seq * D_GROUP_CH)
    tmat = jnp.concatenate([ct, -st], axis=1).astype(BF16)
    return (cc * norm).astype(BF16), (sc * norm).astype(BF16), tmat


def _fourier_mix(u, tables, n_batch, seq, row0, name):
    cc, sc, tmat = tables
    tr = min(DFT_ROW_TILE, seq)
    n_r = seq // tr
    return pl.pallas_call(
        functools.partial(_dft_kernel, seq=seq),
        grid=(n_batch, n_r),
        in_specs=[
            pl.BlockSpec((seq, BRANCH_W), lambda b, r: (row0 // seq + b, 0)),
            pl.BlockSpec((LANES, LANES), lambda b, r: (0, 0)),
            pl.BlockSpec((LANES, LANES), lambda b, r: (0, 0)),
            pl.BlockSpec((tr, 2 * seq), lambda b, r: (r, 0)),
        ],
        out_specs=pl.BlockSpec((tr, BRANCH_W), lambda b, r: (b * n_r + r, 0)),
        out_shape=jax.ShapeDtypeStruct((n_batch * seq, BRANCH_W), BF16),
        scratch_shapes=[pltpu.VMEM((2 * seq, BRANCH_W), BF16)],
        compiler_params=_cparams(("parallel", "arbitrary")),
        name=name,
    )(u, cc, sc, tmat)


def _merge_kernel(ya_ref, yb_ref, yc_ref, yd_ref, gt_ref, h_ref, mod_ref, wb_ref, wo_ref, o_ref):
    acc = None
    for j, y_ref in enumerate((ya_ref, yb_ref, yc_ref, yd_ref)):
        proj = jnp.dot(y_ref[...], wb_ref[j], preferred_element_type=F32)
        term = gt_ref[:, j * D_MODEL:(j + 1) * D_MODEL].astype(F32) * proj
        acc = term if acc is None else acc + term
    y = jnp.dot(acc.astype(BF16), wo_ref[...], preferred_element_type=F32)
    o_ref[...] = h_ref[...] + mod_ref[2:3, :] * y


def _merge(ys, gates, h, mods, layer, wb, wo, n_tiles, lat_tiles, tiles_per_batch, ctx_row):
    tm = TOKEN_TILE
    row = lambda wd: pl.BlockSpec((tm, wd), lambda i: (i, 0))
    return pl.pallas_call(
        _merge_kernel,
        grid=(n_tiles,),
        in_specs=[row(BRANCH_W)] * 4 + [
            row(N_BRANCH * D_MODEL), row(D_MODEL),
            _mod_spec(layer, 1, lat_tiles, tiles_per_batch, ctx_row),
            _const_spec((N_BRANCH, BRANCH_W, D_MODEL)),
            _const_spec((D_MODEL, D_MODEL)),
        ],
        out_specs=row(D_MODEL),
        out_shape=jax.ShapeDtypeStruct((n_tiles * tm, D_MODEL), F32),
        compiler_params=_cparams(("parallel",)),
        name=f"merge_l{layer}",
    )(*ys, gates, h, mods, wb, wo)


def _rope_tables(seq, pad_rows):
    pos = jnp.arange(seq, dtype=jnp.int32)
    row = (pos // GRID_W).astype(F32)
    col = (pos % GRID_W).astype(F32)
    half = HEAD_DIM // 2
    inv = ROPE_BASE ** (-jnp.arange(0, half, 2, dtype=F32) / half)
    ang_r, ang_c = row[:, None] * inv, col[:, None] * inv
    ang = jnp.concatenate([ang_r, ang_r, ang_c, ang_c], axis=-1)
    cos, sin = jnp.cos(ang), jnp.sin(ang)
    lane = jnp.arange(HEAD_DIM)
    sin_s = jnp.where((lane & 16) == 0, -sin, sin)
    cos = jnp.concatenate([cos, jnp.ones((pad_rows, HEAD_DIM), F32)], axis=0)
    sin_s = jnp.concatenate([sin_s, jnp.zeros((pad_rows, HEAD_DIM), F32)], axis=0)
    return jnp.tile(cos, (1, 2)), jnp.tile(sin_s, (1, 2))


def _arrange_in_weight(w_in):
    aq = w_in[:, :BRANCH_W].reshape(D_MODEL, A_KV_HEADS, A_GROUP, HEAD_DIM)
    aq = jnp.transpose(aq, (0, 2, 1, 3)).reshape(D_MODEL, BRANCH_W)
    return jnp.concatenate([aq, w_in[:, BRANCH_W:]], axis=1).astype(BF16)


def _arrange_ffn(w1, w3):
    nch = D_FF // FF_CHUNK
    w13 = jnp.stack([w1.reshape(D_MODEL, nch, FF_CHUNK), w3.reshape(D_MODEL, nch, FF_CHUNK)],
                    axis=2)
    return w13.reshape(D_MODEL, 2 * D_FF).astype(BF16)


def kernel(x, c, ctx, c_ctx, ada_w, ada_b, norm_g, ffn_w1, ffn_w3, ffn_w2, w_in, qk_norm_a,
           conv_w, conv_b, conv_ln_g, conv_ln_b, diff_lam, diff_subln_g, w_branch, w_out, final_g):
    n_batch, seq, _ = x.shape
    ctx_len = ctx.shape[1]
    depth = ada_w.shape[0]
    tm = TOKEN_TILE
    lat_rows, ctx_rows = n_batch * seq, n_batch * ctx_len
    assert seq % tm == 0 and ctx_rows % tm == 0 and ctx_len % CONV_TILE == 0
    assert n_batch < COND_ROWS and seq % GRID_W == 0
    lat_tiles, all_tiles = lat_rows // tm, (lat_rows + ctx_rows) // tm
    tiles_per_batch = seq // tm
    ctx_row = n_batch

    cond = jnp.concatenate(
        [c, c_ctx[None, :], jnp.zeros((COND_ROWS - n_batch - 1, D_MODEL), F32)], axis=0)
    mods = _adaln(cond, ada_w, ada_b).reshape(depth, COND_ROWS, 3, 3, D_MODEL)

    cos_t, sin_t = _rope_tables(seq, tm)
    lane = jnp.arange(LANES)
    seg = (lane[:, None] // HEAD_DIM == lane[None, :] // HEAD_DIM).astype(BF16)
    dft_lat = _dft_tables(seq)
    dft_ctx = _dft_tables(ctx_len)
    tile_args = (lat_tiles, tiles_per_batch, ctx_row)

    h = jnp.concatenate([x.reshape(lat_rows, D_MODEL), ctx.reshape(ctx_rows, D_MODEL)], axis=0)
    for l in range(depth):
        update_ctx = l < depth - 1
        lam_init = 0.8 - 0.6 * math.exp(-0.3 * l)
        out_tiles = all_tiles if update_ctx else lat_tiles
        out_rows = out_tiles * tm
        w13a, w13b = _arrange_ffn(ffn_w1[l, 0], ffn_w3[l, 0]), _arrange_ffn(ffn_w1[l, 1], ffn_w3[l, 1])
        w2a, w2b = ffn_w2[l, 0].astype(BF16), ffn_w2[l, 1].astype(BF16)
        wb_a = w_branch[l, 0].reshape(A_KV_HEADS, A_GROUP, HEAD_DIM, D_MODEL)
        wb_a = jnp.transpose(wb_a, (1, 0, 2, 3)).reshape(BRANCH_W, D_MODEL)
        wb = jnp.concatenate([wb_a[None], w_branch[l, 1:]], axis=0).astype(BF16)
        qkg = jnp.tile(qk_norm_a[l], (1, 2))

        h = _half_ffn(h, mods, l, 0, norm_g[l, 0], w13a, w2a, all_tiles, *tile_args)
        qa, ka, va, yg, qc, kc, vc, du, gates = _in_proj(
            h, mods, l, norm_g[l, 1], _arrange_in_weight(w_in[l]), cos_t, sin_t, seg, qkg,
            all_tiles, *tile_args)

        lat_src = ((0, seq), (lat_rows, ctx_len))
        ctx_src = ((lat_rows, ctx_len),)
        diff_extra = (diff_lam[l], diff_subln_g[l].reshape(1, LANES))
        y_a = _attention("gqa", qa, ka, va, None, n_batch, seq, 0, lat_src, name=f"gqa_l{l}")
        y_c = _attention("diff", qc, kc, vc, diff_extra, n_batch, seq, 0, lat_src,
                         lam_init=lam_init, name=f"diff_l{l}")
        y_b = _conv_module(yg, conv_w[l], conv_b[l], conv_ln_g[l], conv_ln_b[l],
                           out_rows, lat_rows, seq, ctx_len)
        y_d = _fourier_mix(du, dft_lat, n_batch, seq, 0, f"dft_l{l}")
        if update_ctx:
            y_a = jnp.concatenate([y_a, _attention(
                "gqa", qa, ka, va, None, n_batch, ctx_len, lat_rows, ctx_src,
                name=f"gqa_ctx_l{l}")], axis=0)
            y_c = jnp.concatenate([y_c, _attention(
                "diff", qc, kc, vc, diff_extra, n_batch, ctx_len, lat_rows, ctx_src,
                lam_init=lam_init, name=f"diff_ctx_l{l}")], axis=0)
            y_d = jnp.concatenate(
                [y_d, _fourier_mix(du, dft_ctx, n_batch, ctx_len, lat_rows, f"dft_ctx_l{l}")],
                axis=0)
        h = _merge((y_a, y_b, y_c, y_d), gates, h, mods, l, wb, w_out[l].astype(BF16),
                   out_tiles, *tile_args)
        h = _half_ffn(h, mods, l, 2, norm_g[l, 2], w13b, w2b, out_tiles, *tile_args,
                      final_g=None if update_ctx else final_g)
    return h.reshape(n_batch, seq, D_MODEL)
```

```python
import functools
import math

import jax
import jax.numpy as jnp
from jax import lax
from jax.experimental import pallas as pl
from jax.experimental.pallas import tpu as pltpu

F32 = jnp.float32
BF16 = jnp.bfloat16

D_MODEL = 1024
GRID_W = 64
HEAD_DIM = 64
ROPE_BASE = 10000.0
EPS = 1e-6
N_MOD = 9
N_BRANCH = 4
BRANCH_W = D_MODEL // 2
A_Q_HEADS = BRANCH_W // HEAD_DIM
A_KV_HEADS = 2
A_GROUP = A_Q_HEADS // A_KV_HEADS
CONV_W = 31
C_HEADS = BRANCH_W // (2 * HEAD_DIM)
D_GROUPS = 4
D_GROUP_CH = BRANCH_W // D_GROUPS
D_FF = 256 * ((8 * D_MODEL // 3 + 255) // 256)
ATTN_SCALE = HEAD_DIM ** -0.5

LANES = 128
COND_ROWS = 16
TOKEN_TILE = 512
FF_CHUNK = 256
CONV_TILE = 256
CONV_HALO = 16
CONV_ROWS = 32
ATTN_Q_ROWS = 512
ATTN_KEY_CHUNK = 1024
Q_SCALE = ATTN_SCALE * math.log2(math.e)
DFT_ROW_TILE = 512
VMEM_LIMIT = 56 * 1024 * 1024


def _cparams(sem):
    return pltpu.CompilerParams(dimension_semantics=sem, vmem_limit_bytes=VMEM_LIMIT)


def _sigmoid(x):
    return 1.0 / (1.0 + jnp.exp(-x))


def _const_spec(shape):
    nd = len(shape)
    return pl.BlockSpec(shape, lambda *_: (0,) * nd, pipeline_mode=pl.Buffered(1))


def _ada_kernel(cond_ref, w_ref, b_ref, o_ref):
    cnd = cond_ref[...]
    s = cnd * _sigmoid(cnd)
    o_ref[...] = jnp.dot(s, w_ref[...], preferred_element_type=F32,
                         precision=lax.Precision.HIGHEST) + b_ref[...]


def _adaln(cond, ada_w, ada_b):
    depth = ada_w.shape[0]
    n_out = N_MOD * D_MODEL
    col = D_MODEL
    return pl.pallas_call(
        _ada_kernel,
        grid=(depth, n_out // col),
        in_specs=[
            pl.BlockSpec((COND_ROWS, D_MODEL), lambda l, j: (0, 0)),
            pl.BlockSpec((None, D_MODEL, col), lambda l, j: (l, 0, j)),
            pl.BlockSpec((None, 1, col), lambda l, j: (l, 0, j)),
        ],
        out_specs=pl.BlockSpec((None, COND_ROWS, col), lambda l, j: (l, 0, j)),
        out_shape=jax.ShapeDtypeStruct((depth, COND_ROWS, n_out), F32),
        compiler_params=_cparams(("arbitrary", "arbitrary")),
        name="adaln",
    )(cond, ada_w, ada_b.reshape(depth, 1, n_out))


def _rms(x, g):
    return x * lax.rsqrt(jnp.mean(x * x, axis=-1, keepdims=True) + EPS) * g


def _mod_spec(layer, sub, lat_tiles, tiles_per_batch, ctx_row):
    def idx(i):
        row = jnp.where(i < lat_tiles, i // tiles_per_batch, ctx_row)
        return (layer, row, sub, 0, 0)
    return pl.BlockSpec((None, None, None, 3, D_MODEL), idx)


def _ffn_kernel(h_ref, mod_ref, g_ref, w13_ref, w2_ref, *rest, final):
    if final:
        fg_ref, o_ref = rest
    else:
        (o_ref,) = rest
    x = h_ref[...]
    shift, scale, gate = mod_ref[0:1, :], mod_ref[1:2, :], mod_ref[2:3, :]
    n = (_rms(x, g_ref[...]) * (1.0 + scale) + shift).astype(BF16)
    acc = jnp.zeros(x.shape, F32)
    c = FF_CHUNK
    for j in range(D_FF // c):
        ab = jnp.dot(n, w13_ref[:, 2 * c * j:2 * c * (j + 1)], preferred_element_type=F32)
        a, b = ab[:, :c], ab[:, c:]
        p = (a * _sigmoid(a) * b).astype(BF16)
        acc = acc + jnp.dot(p, w2_ref[c * j:c * (j + 1), :], preferred_element_type=F32)
    out = x + 0.5 * gate * acc
    if final:
        out = _rms(out, fg_ref[...])
    o_ref[...] = out


def _half_ffn(h, mods, layer, sub, g, w13, w2, n_tiles, lat_tiles, tiles_per_batch, ctx_row,
              final_g=None):
    tm = TOKEN_TILE
    final = final_g is not None
    in_specs = [
        pl.BlockSpec((tm, D_MODEL), lambda i: (i, 0)),
        _mod_spec(layer, sub, lat_tiles, tiles_per_batch, ctx_row),
        _const_spec((1, D_MODEL)),
        _const_spec((D_MODEL, 2 * D_FF)),
        _const_spec((D_FF, D_MODEL)),
    ]
    args = [h, mods, g.reshape(1, D_MODEL), w13, w2]
    if final:
        in_specs.append(_const_spec((1, D_MODEL)))
        args.append(final_g.reshape(1, D_MODEL))
    return pl.pallas_call(
        functools.partial(_ffn_kernel, final=final),
        grid=(n_tiles,),
        in_specs=in_specs,
        out_specs=pl.BlockSpec((tm, D_MODEL), lambda i: (i, 0)),
        out_shape=jax.ShapeDtypeStruct((n_tiles * tm, D_MODEL), F32),
        compiler_params=_cparams(("parallel",)),
        name=f"ffn_l{layer}_s{sub}",
    )(*args)


_IN_SEGS = (("aq", 512), ("ak", 128), ("av", 128), ("ba", 512), ("bg", 512),
            ("cq", 512), ("ck", 512), ("cv", 512), ("du", 512), ("gl", 4096))
_IN_OFF = {}
_o = 0
for _n, _s in _IN_SEGS:
    _IN_OFF[_n] = (_o, _o + _s)
    _o += _s
IN_COLS = _o


def _rope_blocks(x, cos, sin_s, low16):
    outs = []
    for j in range(x.shape[1] // LANES):
        xb = x[:, j * LANES:(j + 1) * LANES]
        partner = jnp.where(low16, pltpu.roll(xb, LANES - 16, 1), pltpu.roll(xb, 16, 1))
        outs.append(xb * cos + partner * sin_s)
    return outs


def _in_kernel(h_ref, mod_ref, g_ref, w_ref, cos_ref, sin_ref, seg_ref, qkg_ref,
               qa_ref, ka_ref, va_ref, yg_ref, qc_ref, kc_ref, vc_ref, du_ref, gt_ref):
    x = h_ref[...]
    shift, scale = mod_ref[0:1, :], mod_ref[1:2, :]
    n = (_rms(x, g_ref[...]) * (1.0 + scale) + shift).astype(BF16)

    def proj(name, lo=None, hi=None):
        a, b = _IN_OFF[name]
        if lo is not None:
            a, b = a + lo, a + hi
        return jnp.dot(n, w_ref[:, a:b], preferred_element_type=F32)

    cos, sin_s = cos_ref[...], sin_ref[...]
    lane = lax.broadcasted_iota(jnp.int32, (1, LANES), 1)
    low16 = (lane & 16) == 0
    seg = seg_ref[...]

    def head_norm(z, g):
        outs = []
        for j in range(z.shape[1] // LANES):
            zb = z[:, j * LANES:(j + 1) * LANES]
            ss = jnp.dot((zb * zb).astype(BF16), seg, preferred_element_type=F32)
            outs.append(zb * lax.rsqrt(ss * (1.0 / HEAD_DIM) + EPS) * g)
        return jnp.concatenate(outs, axis=1)

    q = head_norm(proj("aq"), qkg_ref[0:1, :])
    qa_ref[...] = jnp.concatenate(
        [b * Q_SCALE for b in _rope_blocks(q, cos, sin_s, low16)], axis=1).astype(BF16)
    k = head_norm(proj("ak"), qkg_ref[1:2, :])
    ka_ref[...] = _rope_blocks(k, cos, sin_s, low16)[0].astype(BF16)
    va_ref[...] = proj("av").astype(BF16)
    yg_ref[...] = proj("ba") * _sigmoid(proj("bg"))
    qc_ref[...] = jnp.concatenate(
        [b * Q_SCALE for b in _rope_blocks(proj("cq"), cos, sin_s, low16)], axis=1).astype(BF16)
    kc_ref[...] = jnp.concatenate(_rope_blocks(proj("ck"), cos, sin_s, low16), axis=1).astype(BF16)
    vc_ref[...] = proj("cv").astype(BF16)
    du_ref[...] = proj("du").astype(BF16)
    for j in range(N_BRANCH):
        gt_ref[:, j * D_MODEL:(j + 1) * D_MODEL] = _sigmoid(
            proj("gl", j * D_MODEL, (j + 1) * D_MODEL)).astype(BF16)


def _in_proj(h, mods, layer, g, w, cos_t, sin_t, seg, qkg, n_tiles, lat_tiles, tiles_per_batch,
             ctx_row):
    tm = TOKEN_TILE
    t = n_tiles * tm

    def rope_idx(i):
        return (jnp.where(i < lat_tiles, i % tiles_per_batch, tiles_per_batch), 0)

    widths = (512, 128, 128, 512, 512, 512, 512, 512, 4096)
    dtypes = (BF16, BF16, BF16, F32, BF16, BF16, BF16, BF16, BF16)
    return pl.pallas_call(
        _in_kernel,
        grid=(n_tiles,),
        in_specs=[
            pl.BlockSpec((tm, D_MODEL), lambda i: (i, 0)),
            _mod_spec(layer, 1, lat_tiles, tiles_per_batch, ctx_row),
            _const_spec((1, D_MODEL)),
            _const_spec((D_MODEL, IN_COLS)),
            pl.BlockSpec((tm, LANES), rope_idx),
            pl.BlockSpec((tm, LANES), rope_idx),
            _const_spec((LANES, LANES)),
            _const_spec((2, LANES)),
        ],
        out_specs=[pl.BlockSpec((tm, wd), lambda i: (i, 0)) for wd in widths],
        out_shape=[jax.ShapeDtypeStruct((t, wd), dt) for wd, dt in zip(widths, dtypes)],
        compiler_params=_cparams(("parallel",)),
        name=f"in_proj_l{layer}",
    )(h, mods, g.reshape(1, D_MODEL), w, cos_t, sin_t, seg, qkg)


def _attn_kernel(*refs, kind, key_counts, n_units, lam_init):
    n_src = len(key_counts)
    q_ref = refs[0]
    k_refs, v_refs = refs[1:1 + n_src], refs[1 + n_src:1 + 2 * n_src]
    rest = refs[1 + 2 * n_src:]
    if kind == "gqa":
        o_ref, s_a, s_b, mx_a, mx_b = rest
    else:
        lam_ref, g_ref, o_ref, s_a, s_b, mx_a, mx_b, o0_scr = rest
    t = pl.program_id(0)

    @pl.when(t == 0)
    def _():
        s_b[...] = jnp.zeros(s_b.shape, F32)
        mx_b[...] = jnp.zeros(mx_b.shape, F32)
        if kind == "gqa":
            o_ref[...] = jnp.zeros(o_ref.shape, BF16)
        else:
            o0_scr[...] = jnp.zeros(o0_scr.shape, F32)

    def step(front_half, s_front, mx_front, s_back, mx_back):
        lane_half = lax.broadcasted_iota(jnp.int32, (1, LANES), 1) // HEAD_DIM
        qp = jnp.where(lane_half == front_half, q_ref[...].astype(F32), 0.0).astype(BF16)
        m_back = mx_back[:, 0:1]
        run_max, acc, off = None, None, 0
        for k_ref, v_ref, n_keys in zip(k_refs, v_refs, key_counts):
            for c0 in range(0, n_keys, ATTN_KEY_CHUNK):
                kc = min(ATTN_KEY_CHUNK, n_keys - c0)
                s = lax.dot_general(qp, k_ref[c0:c0 + kc, :], (((1,), (1,)), ((), ())),
                                    preferred_element_type=F32)
                s_front[:, off:off + kc] = s
                part = functools.reduce(
                    jnp.maximum, [s[:, j * LANES:(j + 1) * LANES] for j in range(kc // LANES)])
                run_max = part if run_max is None else jnp.maximum(run_max, part)
                p = jnp.exp2(s_back[:, off:off + kc] - m_back).astype(BF16)
                v_ext = jnp.concatenate(
                    [v_ref[c0:c0 + kc, :], jnp.ones((kc, LANES), BF16)], axis=1)
                pv = jnp.dot(p, v_ext, preferred_element_type=F32)
                acc = pv if acc is None else acc + pv
                off += kc
        mx_front[...] = jnp.broadcast_to(jnp.max(run_max, axis=-1, keepdims=True), run_max.shape)
        o = acc[:, :LANES] / acc[:, LANES:]
        back_half = 1 - front_half
        if kind == "gqa":
            if back_half == 0:
                o_ref[...] = o.astype(BF16)
            else:
                low = lax.broadcasted_iota(jnp.int32, (1, LANES), 1) < HEAD_DIM
                o_ref[...] = jnp.where(low, o_ref[...], o.astype(BF16))
        elif back_half == 0:
            o0_scr[...] = o
        else:
            lp = lam_ref[...]
            lam = (jnp.exp(jnp.sum(lp[0:1, :] * lp[1:2, :], axis=-1, keepdims=True))
                   - jnp.exp(jnp.sum(lp[2:3, :] * lp[3:4, :], axis=-1, keepdims=True)) + lam_init)
            o_ref[...] = (_rms(o0_scr[...] - lam * o, g_ref[...]) * (1.0 - lam_init)).astype(BF16)

    @pl.when(t % 2 == 0)
    def _():
        step(0, s_a, mx_a, s_b, mx_b)

    @pl.when(t % 2 == 1)
    def _():
        step(1, s_b, mx_b, s_a, mx_a)


def _attention(kind, q, k, v, extra, n_batch, q_rows, q_row0, sources, lam_init=None, name=""):
    tq = min(ATTN_Q_ROWS, q_rows)
    n_q = q_rows // tq
    per_tile = 2 * BRANCH_W // LANES
    n_units = n_batch * n_q * per_tile
    kv_col = (lambda blk: 0) if kind == "gqa" else (lambda blk: blk)

    def decode(x):
        return x // (n_q * per_tile), (x // per_tile) % n_q, (x % per_tile) // 2

    front = lambda t: decode(jnp.minimum(t, n_units - 1))
    back = lambda t: decode(jnp.maximum(t - 1, 0))

    def q_idx(t):
        b, i, blk = front(t)
        return (q_row0 // tq + b * n_q + i, blk)

    def o_idx(t):
        b, i, blk = back(t)
        return (b * n_q + i, blk)

    def kv_idx(t, which, r):
        b, _, blk = which(t)
        return (r + b, kv_col(blk))

    in_specs = [pl.BlockSpec((tq, LANES), q_idx)]
    args = [q]
    for arr, which in ((k, front), (v, back)):
        for row0, n_keys in sources:
            in_specs.append(pl.BlockSpec(
                (n_keys, LANES), functools.partial(kv_idx, which=which, r=row0 // n_keys)))
            args.append(arr)
    total_keys = sum(nk for _, nk in sources)
    scratch = [pltpu.VMEM((tq, total_keys), F32), pltpu.VMEM((tq, total_keys), F32),
               pltpu.VMEM((tq, LANES), F32), pltpu.VMEM((tq, LANES), F32)]
    if kind == "diff":
        in_specs += [pl.BlockSpec(extra[0].shape, lambda t: (0, 0)),
                     pl.BlockSpec(extra[1].shape, lambda t: (0, 0))]
        args += list(extra)
        scratch.append(pltpu.VMEM((tq, LANES), F32))
    return pl.pallas_call(
        functools.partial(_attn_kernel, kind=kind, key_counts=tuple(nk for _, nk in sources),
                          n_units=n_units, lam_init=lam_init),
        grid=(n_units + 1,),
        in_specs=in_specs,
        out_specs=pl.BlockSpec((tq, LANES), o_idx),
        out_shape=jax.ShapeDtypeStruct((n_batch * q_rows, BRANCH_W), BF16),
        scratch_shapes=scratch,
        compiler_params=_cparams(("arbitrary",)),
        name=name,
    )(*args)


def _conv_kernel(prev_ref, cur_ref, next_ref, w_ref, b_ref, g_ref, beta_ref, o_ref, win_ref,
                 *, lat_tiles, lat_per_seq, ctx_per_seq):
    i = pl.program_id(0)
    j = jnp.where(i < lat_tiles, i % lat_per_seq, (i - lat_tiles) % ctx_per_seq)
    per = jnp.where(i < lat_tiles, lat_per_seq, ctx_per_seq)
    first, last = j == 0, j == per - 1
    hl, tm = CONV_HALO, CONV_TILE
    win_ref[0:hl, :] = jnp.where(first, 0.0, prev_ref[...])
    win_ref[hl:hl + tm, :] = cur_ref[...]
    win_ref[hl + tm:hl + tm + hl, :] = jnp.where(last, 0.0, next_ref[...])
    half = CONV_W // 2
    rc = CONV_ROWS
    for c in range(tm // rc):
        acc = jnp.zeros((rc, BRANCH_W), F32)
        for k in range(CONV_W):
            r0 = c * rc + k + hl - half
            acc = acc + win_ref[r0:r0 + rc, :] * w_ref[k:k + 1, :]
        y = acc + b_ref[...]
        mu = jnp.mean(y, axis=-1, keepdims=True)
        yc = y - mu
        var = jnp.mean(yc * yc, axis=-1, keepdims=True)
        yn = yc * lax.rsqrt(var + EPS) * g_ref[...] + beta_ref[...]
        o_ref[c * rc:(c + 1) * rc, :] = (yn * _sigmoid(yn)).astype(BF16)


def _conv_module(y, w, b, g, beta, n_rows, lat_rows, seq, ctx_len):
    tm, hl = CONV_TILE, CONV_HALO
    n_tiles = n_rows // tm
    r = tm // hl
    n_halo = y.shape[0] // hl
    body = functools.partial(_conv_kernel, lat_tiles=lat_rows // tm, lat_per_seq=seq // tm,
                             ctx_per_seq=max(ctx_len // tm, 1))
    vec = lambda a: a.reshape(1, BRANCH_W)
    return pl.pallas_call(
        body,
        grid=(n_tiles,),
        in_specs=[
            pl.BlockSpec((hl, BRANCH_W), lambda i: (jnp.maximum(i * r - 1, 0), 0)),
            pl.BlockSpec((tm, BRANCH_W), lambda i: (i, 0)),
            pl.BlockSpec((hl, BRANCH_W), lambda i: (jnp.minimum((i + 1) * r, n_halo - 1), 0)),
            _const_spec((CONV_W, BRANCH_W)),
            _const_spec((1, BRANCH_W)), _const_spec((1, BRANCH_W)), _const_spec((1, BRANCH_W)),
        ],
        out_specs=pl.BlockSpec((tm, BRANCH_W), lambda i: (i, 0)),
        out_shape=jax.ShapeDtypeStruct((n_rows, BRANCH_W), BF16),
        scratch_shapes=[pltpu.VMEM((tm + 2 * hl, BRANCH_W), F32)],
        compiler_params=_cparams(("parallel",)),
        name="conv_module",
    )(y, y, y, w, vec(b), vec(g), vec(beta))


def _dft_kernel(u_ref, cc_ref, sc_ref, t_ref, o_ref, z_ref, *, seq):
    @pl.when(pl.program_id(1) == 0)
    def _():
        for gidx in range(D_GROUPS):
            ub = u_ref[:, gidx * LANES:(gidx + 1) * LANES]
            z_ref[0:seq, gidx * LANES:(gidx + 1) * LANES] = jnp.dot(
                ub, cc_ref[...], preferred_element_type=F32).astype(BF16)
            z_ref[seq:2 * seq, gidx * LANES:(gidx + 1) * LANES] = jnp.dot(
                ub, sc_ref[...], preferred_element_type=F32).astype(BF16)
    o_ref[...] = jnp.dot(t_ref[...], z_ref[...], preferred_element_type=F32).astype(BF16)


def _dft_tables(seq):
    def cs(n):
        idx = jnp.arange(n, dtype=jnp.int32)
        ang = ((idx[:, None] * idx[None, :]) % n).astype(F32) * (2.0 * math.pi / n)
        return jnp.cos(ang), jnp.sin(ang)
    ct, st = cs(seq)
    cc, sc = cs(D_GROUP_CH)
    norm = 1.0 / math.sqrt(seq * D_GROUP_CH)
    tmat = jnp.concatenate([ct, -st], axis=1).astype(BF16)
    return (cc * norm).astype(BF16), (sc * norm).astype(BF16), tmat


def _fourier_mix(u, tables, n_batch, seq, row0, name):
    cc, sc, tmat = tables
    tr = min(DFT_ROW_TILE, seq)
    n_r = seq // tr
    return pl.pallas_call(
        functools.partial(_dft_kernel, seq=seq),
        grid=(n_batch, n_r),
        in_specs=[
            pl.BlockSpec((seq, BRANCH_W), lambda b, r: (row0 // seq + b, 0)),
            pl.BlockSpec((LANES, LANES), lambda b, r: (0, 0)),
            pl.BlockSpec((LANES, LANES), lambda b, r: (0, 0)),
            pl.BlockSpec((tr, 2 * seq), lambda b, r: (r, 0)),
        ],
        out_specs=pl.BlockSpec((tr, BRANCH_W), lambda b, r: (b * n_r + r, 0)),
        out_shape=jax.ShapeDtypeStruct((n_batch * seq, BRANCH_W), BF16),
        scratch_shapes=[pltpu.VMEM((2 * seq, BRANCH_W), BF16)],
        compiler_params=_cparams(("parallel", "arbitrary")),
        name=name,
    )(u, cc, sc, tmat)


def _merge_kernel(ya_ref, yb_ref, yc_ref, yd_ref, gt_ref, h_ref, mod_ref, wb_ref, wo_ref, o_ref):
    acc = None
    for j, y_ref in enumerate((ya_ref, yb_ref, yc_ref, yd_ref)):
        proj = jnp.dot(y_ref[...], wb_ref[j], preferred_element_type=F32)
        term = gt_ref[:, j * D_MODEL:(j + 1) * D_MODEL].astype(F32) * proj
        acc = term if acc is None else acc + term
    y = jnp.dot(acc.astype(BF16), wo_ref[...], preferred_element_type=F32)
    o_ref[...] = h_ref[...] + mod_ref[2:3, :] * y


def _merge(ys, gates, h, mods, layer, wb, wo, n_tiles, lat_tiles, tiles_per_batch, ctx_row):
    tm = TOKEN_TILE
    row = lambda wd: pl.BlockSpec((tm, wd), lambda i: (i, 0))
    return pl.pallas_call(
        _merge_kernel,
        grid=(n_tiles,),
        in_specs=[row(BRANCH_W)] * 4 + [
            row(N_BRANCH * D_MODEL), row(D_MODEL),
            _mod_spec(layer, 1, lat_tiles, tiles_per_batch, ctx_row),
            _const_spec((N_BRANCH, BRANCH_W, D_MODEL)),
            _const_spec((D_MODEL, D_MODEL)),
        ],
        out_specs=row(D_MODEL),
        out_shape=jax.ShapeDtypeStruct((n_tiles * tm, D_MODEL), F32),
        compiler_params=_cparams(("parallel",)),
        name=f"merge_l{layer}",
    )(*ys, gates, h, mods, wb, wo)


def _rope_tables(seq, pad_rows):
    pos = jnp.arange(seq, dtype=jnp.int32)
    row = (pos // GRID_W).astype(F32)
    col = (pos % GRID_W).astype(F32)
    half = HEAD_DIM // 2
    inv = ROPE_BASE ** (-jnp.arange(0, half, 2, dtype=F32) / half)
    ang_r, ang_c = row[:, None] * inv, col[:, None] * inv
    ang = jnp.concatenate([ang_r, ang_r, ang_c, ang_c], axis=-1)
    cos, sin = jnp.cos(ang), jnp.sin(ang)
    lane = jnp.arange(HEAD_DIM)
    sin_s = jnp.where((lane & 16) == 0, -sin, sin)
    cos = jnp.concatenate([cos, jnp.ones((pad_rows, HEAD_DIM), F32)], axis=0)
    sin_s = jnp.concatenate([sin_s, jnp.zeros((pad_rows, HEAD_DIM), F32)], axis=0)
    return jnp.tile(cos, (1, 2)), jnp.tile(sin_s, (1, 2))


def _arrange_in_weight(w_in):
    aq = w_in[:, :BRANCH_W].reshape(D_MODEL, A_KV_HEADS, A_GROUP, HEAD_DIM)
    aq = jnp.transpose(aq, (0, 2, 1, 3)).reshape(D_MODEL, BRANCH_W)
    return jnp.concatenate([aq, w_in[:, BRANCH_W:]], axis=1).astype(BF16)


def _arrange_ffn(w1, w3):
    nch = D_FF // FF_CHUNK
    w13 = jnp.stack([w1.reshape(D_MODEL, nch, FF_CHUNK), w3.reshape(D_MODEL, nch, FF_CHUNK)],
                    axis=2)
    return w13.reshape(D_MODEL, 2 * D_FF).astype(BF16)


def kernel(x, c, ctx, c_ctx, ada_w, ada_b, norm_g, ffn_w1, ffn_w3, ffn_w2, w_in, qk_norm_a,
           conv_w, conv_b, conv_ln_g, conv_ln_b, diff_lam, diff_subln_g, w_branch, w_out, final_g):
    n_batch, seq, _ = x.shape
    ctx_len = ctx.shape[1]
    depth = ada_w.shape[0]
    tm = TOKEN_TILE
    lat_rows, ctx_rows = n_batch * seq, n_batch * ctx_len
    assert seq % tm == 0 and ctx_rows % tm == 0 and ctx_len % CONV_TILE == 0
    assert n_batch < COND_ROWS and seq % GRID_W == 0
    lat_tiles, all_tiles = lat_rows // tm, (lat_rows + ctx_rows) // tm
    tiles_per_batch = seq // tm
    ctx_row = n_batch

    cond = jnp.concatenate(
        [c, c_ctx[None, :], jnp.zeros((COND_ROWS - n_batch - 1, D_MODEL), F32)], axis=0)
    mods = _adaln(cond, ada_w, ada_b).reshape(depth, COND_ROWS, 3, 3, D_MODEL)

    cos_t, sin_t = _rope_tables(seq, tm)
    lane = jnp.arange(LANES)
    seg = (lane[:, None] // HEAD_DIM == lane[None, :] // HEAD_DIM).astype(BF16)
    dft_lat = _dft_tables(seq)
    dft_ctx = _dft_tables(ctx_len)
    tile_args = (lat_tiles, tiles_per_batch, ctx_row)

    h = jnp.concatenate([x.reshape(lat_rows, D_MODEL), ctx.reshape(ctx_rows, D_MODEL)], axis=0)
    for l in range(depth):
        update_ctx = l < depth - 1
        lam_init = 0.8 - 0.6 * math.exp(-0.3 * l)
        out_tiles = all_tiles if update_ctx else lat_tiles
        out_rows = out_tiles * tm
        w13a, w13b = _arrange_ffn(ffn_w1[l, 0], ffn_w3[l, 0]), _arrange_ffn(ffn_w1[l, 1], ffn_w3[l, 1])
        w2a, w2b = ffn_w2[l, 0].astype(BF16), ffn_w2[l, 1].astype(BF16)
        wb_a = w_branch[l, 0].reshape(A_KV_HEADS, A_GROUP, HEAD_DIM, D_MODEL)
        wb_a = jnp.transpose(wb_a, (1, 0, 2, 3)).reshape(BRANCH_W, D_MODEL)
        wb = jnp.concatenate([wb_a[None], w_branch[l, 1:]], axis=0).astype(BF16)
        qkg = jnp.tile(qk_norm_a[l], (1, 2))

        h = _half_ffn(h, mods, l, 0, norm_g[l, 0], w13a, w2a, all_tiles, *tile_args)
        qa, ka, va, yg, qc, kc, vc, du, gates = _in_proj(
            h, mods, l, norm_g[l, 1], _arrange_in_weight(w_in[l]), cos_t, sin_t, seg, qkg,
            all_tiles, *tile_args)

        lat_src = ((0, seq), (lat_rows, ctx_len))
        ctx_src = ((lat_rows, ctx_len),)
        diff_extra = (diff_lam[l], diff_subln_g[l].reshape(1, LANES))
        y_a = _attention("gqa", qa, ka, va, None, n_batch, seq, 0, lat_src, name=f"gqa_l{l}")
        y_c = _attention("diff", qc, kc, vc, diff_extra, n_batch, seq, 0, lat_src,
                         lam_init=lam_init, name=f"diff_l{l}")
        y_b = _conv_module(yg, conv_w[l], conv_b[l], conv_ln_g[l], conv_ln_b[l],
                           out_rows, lat_rows, seq, ctx_len)
        y_d = _fourier_mix(du, dft_lat, n_batch, seq, 0, f"dft_l{l}")
        if update_ctx:
            y_a = jnp.concatenate([y_a, _attention(
                "gqa", qa, ka, va, None, n_batch, ctx_len, lat_rows, ctx_src,
                name=f"gqa_ctx_l{l}")], axis=0)
            y_c = jnp.concatenate([y_c, _attention(
                "diff", qc, kc, vc, diff_extra, n_batch, ctx_len, lat_rows, ctx_src,
                lam_init=lam_init, name=f"diff_ctx_l{l}")], axis=0)
            y_d = jnp.concatenate(
                [y_d, _fourier_mix(du, dft_ctx, n_batch, ctx_len, lat_rows, f"dft_ctx_l{l}")],
                axis=0)
        h = _merge((y_a, y_b, y_c, y_d), gates, h, mods, l, wb, w_out[l].astype(BF16),
                   out_tiles, *tile_args)
        h = _half_ffn(h, mods, l, 2, norm_g[l, 2], w13b, w2b, out_tiles, *tile_args,
                      final_g=None if update_ctx else final_g)
    return h.reshape(n_batch, seq, D_MODEL)
```

```python
import functools
import math

import jax
import jax.numpy as jnp
from jax import lax
from jax.experimental import pallas as pl
from jax.experimental.pallas import tpu as pltpu

F32 = jnp.float32
BF16 = jnp.bfloat16

D_MODEL = 1024
GRID_W = 64
HEAD_DIM = 64
ROPE_BASE = 10000.0
EPS = 1e-6
N_MOD = 9
N_BRANCH = 4
BRANCH_W = D_MODEL // 2
A_Q_HEADS = BRANCH_W // HEAD_DIM
A_KV_HEADS = 2
A_GROUP = A_Q_HEADS // A_KV_HEADS
CONV_W = 31
C_HEADS = BRANCH_W // (2 * HEAD_DIM)
D_GROUPS = 4
D_GROUP_CH = BRANCH_W // D_GROUPS
D_FF = 256 * ((8 * D_MODEL // 3 + 255) // 256)
ATTN_SCALE = HEAD_DIM ** -0.5

LANES = 128
SUBLANES = 8
COND_ROWS = 16
TOKEN_TILE = 512
FF_CHUNK = 256
CONV_TILE = 256
CONV_HALO = 16
CONV_ROWS = 32
ATTN_Q_ROWS = 1024
ATTN_KEY_CHUNK = 1024
Q_SCALE = ATTN_SCALE * math.log2(math.e)
DFT_ROW_TILE = 512
VMEM_LIMIT = 56 * 1024 * 1024


def _cparams(sem):
    return pltpu.CompilerParams(dimension_semantics=sem, vmem_limit_bytes=VMEM_LIMIT)


def _sigmoid(x):
    return 1.0 / (1.0 + jnp.exp(-x))


def _const_spec(shape, lead=()):
    nd = len(shape)
    return pl.BlockSpec((None,) * len(lead) + tuple(shape), lambda *_: tuple(lead) + (0,) * nd,
                        pipeline_mode=pl.Buffered(1))


def _ada_kernel(cond_ref, w_ref, b_ref, o_ref):
    cnd = cond_ref[...]
    s = cnd * _sigmoid(cnd)
    o_ref[...] = jnp.dot(s, w_ref[...], preferred_element_type=F32,
                         precision=lax.Precision.HIGHEST) + b_ref[...]


def _adaln(cond, ada_w, ada_b):
    depth = ada_w.shape[0]
    n_out = N_MOD * D_MODEL
    col = D_MODEL
    return pl.pallas_call(
        _ada_kernel,
        grid=(depth, n_out // col),
        in_specs=[
            pl.BlockSpec((COND_ROWS, D_MODEL), lambda l, j: (0, 0)),
            pl.BlockSpec((None, D_MODEL, col), lambda l, j: (l, 0, j)),
            pl.BlockSpec((None, 1, col), lambda l, j: (l, 0, j)),
        ],
        out_specs=pl.BlockSpec((None, COND_ROWS, col), lambda l, j: (l, 0, j)),
        out_shape=jax.ShapeDtypeStruct((depth, COND_ROWS, n_out), F32),
        compiler_params=_cparams(("arbitrary", "arbitrary")),
        name="adaln",
    )(cond, ada_w, ada_b.reshape(depth, 1, n_out))


def _rms(x, g):
    return x * lax.rsqrt(jnp.mean(x * x, axis=-1, keepdims=True) + EPS) * g


def _mod_spec(layer, sub, lat_tiles, tiles_per_batch, ctx_row):
    def idx(i):
        row = jnp.where(i < lat_tiles, i // tiles_per_batch, ctx_row)
        return (layer, row, sub, 0, 0)
    return pl.BlockSpec((None, None, None, 3, D_MODEL), idx)


def _ffn_kernel(*refs, final, lat_tiles, split):
    refs = list(refs)
    h_ref = refs.pop(0)
    ctx_ref = refs.pop(0) if split else None
    mod_ref, g_ref, w1_ref, w3_ref, w2_ref = refs[:5]
    fg_ref = refs[5] if final else None
    o_ref = refs[-1]
    x = h_ref[...]
    if split:
        x = jnp.where(pl.program_id(0) < lat_tiles, x, ctx_ref[...])
    shift, scale, gate = mod_ref[0:1, :], mod_ref[1:2, :], mod_ref[2:3, :]
    n = (_rms(x, g_ref[...]) * (1.0 + scale) + shift).astype(BF16)
    acc = jnp.zeros(x.shape, F32)
    c = FF_CHUNK
    for j in range(D_FF // c):
        a = jnp.dot(n, w1_ref[:, c * j:c * (j + 1)], preferred_element_type=F32)
        b = jnp.dot(n, w3_ref[:, c * j:c * (j + 1)], preferred_element_type=F32)
        p = (a * _sigmoid(a) * b).astype(BF16)
        acc = acc + jnp.dot(p, w2_ref[c * j:c * (j + 1), :], preferred_element_type=F32)
    out = x + 0.5 * gate * acc
    if final:
        out = _rms(out, fg_ref[...])
    o_ref[...] = out


def _half_ffn(h, mods, layer, sub, g, w1, w3, w2, wsel, n_tiles, lat_tiles, tiles_per_batch, ctx_row,
              final_g=None, h_ctx=None):
    tm = TOKEN_TILE
    final = final_g is not None
    split = h_ctx is not None
    if split:
        in_specs = [pl.BlockSpec((tm, D_MODEL), lambda i: (jnp.minimum(i, lat_tiles - 1), 0)),
                    pl.BlockSpec((tm, D_MODEL), lambda i: (jnp.maximum(i - lat_tiles, 0), 0))]
        args = [h, h_ctx]
    else:
        in_specs = [pl.BlockSpec((tm, D_MODEL), lambda i: (i, 0))]
        args = [h]
    in_specs += [
        _mod_spec(layer, sub, lat_tiles, tiles_per_batch, ctx_row),
        _const_spec((1, D_MODEL)),
        _const_spec((D_MODEL, D_FF), wsel),
        _const_spec((D_MODEL, D_FF), wsel),
        _const_spec((D_FF, D_MODEL), wsel),
    ]
    args += [mods, g.reshape(1, D_MODEL), w1, w3, w2]
    if final:
        in_specs.append(_const_spec((1, D_MODEL)))
        args.append(final_g.reshape(1, D_MODEL))
    return pl.pallas_call(
        functools.partial(_ffn_kernel, final=final, lat_tiles=lat_tiles, split=split),
        grid=(n_tiles,),
        in_specs=in_specs,
        out_specs=pl.BlockSpec((tm, D_MODEL), lambda i: (i, 0)),
        out_shape=jax.ShapeDtypeStruct((n_tiles * tm, D_MODEL), F32),
        compiler_params=_cparams(("parallel",)),
        name=f"ffn_l{layer}_s{sub}",
    )(*args)


_IN_SEGS = (("aq", 512), ("ak", 128), ("av", 128), ("ba", 512), ("bg", 512),
            ("cq", 512), ("ck", 512), ("cv", 512), ("du", 512), ("gl", 4096))
_IN_OFF = {}
_o = 0
for _n, _s in _IN_SEGS:
    _IN_OFF[_n] = (_o, _o + _s)
    _o += _s
IN_COLS = _o


def _rope_blocks(x, cos, sin_s, low16):
    outs = []
    for j in range(x.shape[1] // LANES):
        xb = x[:, j * LANES:(j + 1) * LANES]
        partner = jnp.where(low16, pltpu.roll(xb, LANES - 16, 1), pltpu.roll(xb, 16, 1))
        outs.append(xb * cos + partner * sin_s)
    return outs


def _in_kernel(h_ref, mod_ref, g_ref, w_ref, wq_ref, cos_ref, sin_ref, seg_ref, qkg_ref,
               qa_ref, ka_ref, va_ref, yg_ref, qc_ref, kc_ref, vc_ref, du_ref, gt_ref):
    x = h_ref[...]
    shift, scale = mod_ref[0:1, :], mod_ref[1:2, :]
    n = (_rms(x, g_ref[...]) * (1.0 + scale) + shift).astype(BF16)

    def proj(name, lo=None, hi=None):
        if name == "aq":
            return jnp.dot(n, wq_ref[...], preferred_element_type=F32)
        a, b = _IN_OFF[name]
        if lo is not None:
            a, b = a + lo, a + hi
        return jnp.dot(n, w_ref[:, a:b], preferred_element_type=F32)

    cos, sin_s = cos_ref[...], sin_ref[...]
    lane = lax.broadcasted_iota(jnp.int32, (1, LANES), 1)
    low16 = (lane & 16) == 0
    seg = seg_ref[...]

    def head_norm(z, g):
        outs = []
        for j in range(z.shape[1] // LANES):
            zb = z[:, j * LANES:(j + 1) * LANES]
            ss = jnp.dot((zb * zb).astype(BF16), seg, preferred_element_type=F32)
            outs.append(zb * lax.rsqrt(ss * (1.0 / HEAD_DIM) + EPS) * g)
        return jnp.concatenate(outs, axis=1)

    q = head_norm(proj("aq"), qkg_ref[0:1, :])
    qa_ref[...] = jnp.concatenate(
        [b * Q_SCALE for b in _rope_blocks(q, cos, sin_s, low16)], axis=1).astype(BF16)
    k = head_norm(proj("ak"), qkg_ref[1:2, :])
    ka_ref[...] = _rope_blocks(k, cos, sin_s, low16)[0].astype(BF16)
    va_ref[...] = proj("av").astype(BF16)
    yg_ref[...] = proj("ba") * _sigmoid(proj("bg"))
    qc_ref[...] = jnp.concatenate(
        [b * Q_SCALE for b in _rope_blocks(proj("cq"), cos, sin_s, low16)], axis=1).astype(BF16)
    kc_ref[...] = jnp.concatenate(_rope_blocks(proj("ck"), cos, sin_s, low16), axis=1).astype(BF16)
    vc_ref[...] = proj("cv").astype(BF16)
    du_ref[...] = proj("du").astype(BF16)
    for j in range(N_BRANCH):
        gt_ref[:, j * D_MODEL:(j + 1) * D_MODEL] = _sigmoid(
            proj("gl", j * D_MODEL, (j + 1) * D_MODEL)).astype(BF16)


def _in_proj(h, mods, layer, g, w, wq, cos_t, sin_t, seg, qkg, n_tiles, lat_tiles,
             tiles_per_batch, ctx_row):
    tm = TOKEN_TILE
    t = n_tiles * tm

    def rope_idx(i):
        return (jnp.where(i < lat_tiles, i % tiles_per_batch, tiles_per_batch), 0)

    widths = (512, 128, 128, 512, 512, 512, 512, 512, 4096)
    dtypes = (BF16, BF16, BF16, F32, BF16, BF16, BF16, BF16, BF16)
    return pl.pallas_call(
        _in_kernel,
        grid=(n_tiles,),
        in_specs=[
            pl.BlockSpec((tm, D_MODEL), lambda i: (i, 0)),
            _mod_spec(layer, 1, lat_tiles, tiles_per_batch, ctx_row),
            _const_spec((1, D_MODEL)),
            _const_spec((D_MODEL, IN_COLS), (layer,)),
            _const_spec((D_MODEL, BRANCH_W)),
            pl.BlockSpec((tm, LANES), rope_idx),
            pl.BlockSpec((tm, LANES), rope_idx),
            _const_spec((LANES, LANES)),
            _const_spec((2, LANES)),
        ],
        out_specs=[pl.BlockSpec((tm, wd), lambda i: (i, 0)) for wd in widths],
        out_shape=[jax.ShapeDtypeStruct((t, wd), dt) for wd, dt in zip(widths, dtypes)],
        compiler_params=_cparams(("parallel",)),
        name=f"in_proj_l{layer}",
    )(h, mods, g.reshape(1, D_MODEL), w, wq, cos_t, sin_t, seg, qkg)


def _attn_kernel(*refs, kind, key_counts, n_units, lam_init):
    n_src = len(key_counts)
    q_ref = refs[0]
    k_refs, v_refs = refs[1:1 + n_src], refs[1 + n_src:1 + 2 * n_src]
    rest = refs[1 + 2 * n_src:]
    if kind == "gqa":
        o_ref, s_a, s_b, mx_a, mx_b = rest
    else:
        lam_ref, g_ref, o_ref, s_a, s_b, mx_a, mx_b, o0_scr = rest
    t = pl.program_id(0)

    @pl.when(t == 0)
    def _():
        s_b[...] = jnp.zeros(s_b.shape, F32)
        mx_b[...] = jnp.zeros(mx_b.shape, F32)
        if kind == "gqa":
            o_ref[...] = jnp.zeros(o_ref.shape, BF16)
        else:
            o0_scr[...] = jnp.zeros(o0_scr.shape, F32)

    def step(front_half, s_front, mx_front, s_back, mx_back):
        lane_half = lax.broadcasted_iota(jnp.int32, (1, LANES), 1) // HEAD_DIM
        qp = jnp.where(lane_half == front_half, q_ref[...].astype(F32), 0.0).astype(BF16)
        m_back = mx_back[:, 0:1]
        run_max, acc, off = None, None, 0
        for k_ref, v_ref, n_keys in zip(k_refs, v_refs, key_counts):
            for c0 in range(0, n_keys, ATTN_KEY_CHUNK):
                kc = min(ATTN_KEY_CHUNK, n_keys - c0)
                s = lax.dot_general(qp, k_ref[c0:c0 + kc, :], (((1,), (1,)), ((), ())),
                                    preferred_element_type=F32)
                s_front[:, off:off + kc] = s
                part = functools.reduce(
                    jnp.maximum, [s[:, j * LANES:(j + 1) * LANES] for j in range(kc // LANES)])
                run_max = part if run_max is None else jnp.maximum(run_max, part)
                p = jnp.exp2(s_back[:, off:off + kc] - m_back).astype(BF16)
                v_ext = jnp.concatenate(
                    [v_ref[c0:c0 + kc, :], jnp.ones((kc, LANES), BF16)], axis=1)
                pv = jnp.dot(p, v_ext, preferred_element_type=F32)
                acc = pv if acc is None else acc + pv
                off += kc
        mx_front[...] = jnp.broadcast_to(jnp.max(run_max, axis=-1, keepdims=True), run_max.shape)
        o = acc[:, :LANES] / acc[:, LANES:]
        back_half = 1 - front_half
        if kind == "gqa":
            if back_half == 0:
                o_ref[...] = o.astype(BF16)
            else:
                low = lax.broadcasted_iota(jnp.int32, (1, LANES), 1) < HEAD_DIM
                o_ref[...] = jnp.where(low, o_ref[...], o.astype(BF16))
        elif back_half == 0:
            o0_scr[...] = o
        else:
            lp = lam_ref[...]
            lam = (jnp.exp(jnp.sum(lp[0:1, :] * lp[1:2, :], axis=-1, keepdims=True))
                   - jnp.exp(jnp.sum(lp[2:3, :] * lp[3:4, :], axis=-1, keepdims=True)) + lam_init)
            o_ref[...] = (_rms(o0_scr[...] - lam * o, g_ref[...]) * (1.0 - lam_init)).astype(BF16)

    @pl.when(t % 2 == 0)
    def _():
        step(0, s_a, mx_a, s_b, mx_b)

    @pl.when(t % 2 == 1)
    def _():
        step(1, s_b, mx_b, s_a, mx_a)


def _attention(kind, q, k, v, extra, n_batch, q_rows, q_row0, sources, lam_init=None, name=""):
    tq = min(ATTN_Q_ROWS, q_rows)
    n_q = q_rows // tq
    per_tile = 2 * BRANCH_W // LANES
    n_units = n_batch * n_q * per_tile
    kv_col = (lambda blk: 0) if kind == "gqa" else (lambda blk: blk)

    def decode(x):
        return x // (n_q * per_tile), (x // per_tile) % n_q, (x % per_tile) // 2

    front = lambda t: decode(jnp.minimum(t, n_units - 1))
    back = lambda t: decode(jnp.maximum(t - 1, 0))

    def q_idx(t):
        b, i, blk = front(t)
        return (q_row0 // tq + b * n_q + i, blk)

    def o_idx(t):
        b, i, blk = back(t)
        return (b * n_q + i, blk)

    def kv_idx(t, which, r):
        b, _, blk = which(t)
        return (r + b, kv_col(blk))

    in_specs = [pl.BlockSpec((tq, LANES), q_idx)]
    args = [q]
    for arr, which in ((k, front), (v, back)):
        for row0, n_keys in sources:
            in_specs.append(pl.BlockSpec(
                (n_keys, LANES), functools.partial(kv_idx, which=which, r=row0 // n_keys)))
            args.append(arr)
    total_keys = sum(nk for _, nk in sources)
    scratch = [pltpu.VMEM((tq, total_keys), F32), pltpu.VMEM((tq, total_keys), F32),
               pltpu.VMEM((tq, LANES), F32), pltpu.VMEM((tq, LANES), F32)]
    if kind == "diff":
        in_specs += [pl.BlockSpec(extra[0].shape, lambda t: (0, 0)),
                     pl.BlockSpec(extra[1].shape, lambda t: (0, 0))]
        args += list(extra)
        scratch.append(pltpu.VMEM((tq, LANES), F32))
    return pl.pallas_call(
        functools.partial(_attn_kernel, kind=kind, key_counts=tuple(nk for _, nk in sources),
                          n_units=n_units, lam_init=lam_init),
        grid=(n_units + 1,),
        in_specs=in_specs,
        out_specs=pl.BlockSpec((tq, LANES), o_idx),
        out_shape=jax.ShapeDtypeStruct((n_batch * q_rows, BRANCH_W), BF16),
        scratch_shapes=scratch,
        compiler_params=_cparams(("arbitrary",)),
        name=name,
    )(*args)


def _conv_kernel(prev_ref, cur_ref, next_ref, w_ref, b_ref, g_ref, beta_ref, o_ref, win_ref,
                 stage_ref,
                 *, lat_tiles, lat_per_seq, ctx_per_seq):
    i = pl.program_id(0)
    j = jnp.where(i < lat_tiles, i % lat_per_seq, (i - lat_tiles) % ctx_per_seq)
    per = jnp.where(i < lat_tiles, lat_per_seq, ctx_per_seq)
    first, last = j == 0, j == per - 1
    hl, tm = CONV_HALO, CONV_TILE
    win_ref[0:hl, :] = jnp.where(first, 0.0, prev_ref[...])
    win_ref[hl:hl + tm, :] = cur_ref[...]
    win_ref[hl + tm:hl + tm + hl, :] = jnp.where(last, 0.0, next_ref[...])
    half = CONV_W // 2
    rc = CONV_ROWS
    span = stage_ref.shape[1]
    for phase in range(SUBLANES):
        stage_ref[phase] = win_ref[phase:phase + span, :]
    for c in range(tm // rc):
        acc = jnp.zeros((rc // SUBLANES, SUBLANES, BRANCH_W), F32)
        for k in range(CONV_W):
            off = k + hl - half
            r0 = c * rc + off - off % SUBLANES
            xs = stage_ref[off % SUBLANES, r0:r0 + rc, :]
            acc = acc + xs.reshape(rc // SUBLANES, SUBLANES, BRANCH_W) * w_ref[k]
        y = acc.reshape(rc, BRANCH_W) + b_ref[...]
        mu = jnp.mean(y, axis=-1, keepdims=True)
        yc = y - mu
        var = jnp.mean(yc * yc, axis=-1, keepdims=True)
        yn = yc * lax.rsqrt(var + EPS) * g_ref[...] + beta_ref[...]
        o_ref[c * rc:(c + 1) * rc, :] = (yn * _sigmoid(yn)).astype(BF16)


def _conv_module(y, w, b, g, beta, n_rows, lat_rows, seq, ctx_len):
    tm, hl = CONV_TILE, CONV_HALO
    n_tiles = n_rows // tm
    r = tm // hl
    n_halo = y.shape[0] // hl
    body = functools.partial(_conv_kernel, lat_tiles=lat_rows // tm, lat_per_seq=seq // tm,
                             ctx_per_seq=max(ctx_len // tm, 1))
    vec = lambda a: a.reshape(1, BRANCH_W)
    return pl.pallas_call(
        body,
        grid=(n_tiles,),
        in_specs=[
            pl.BlockSpec((hl, BRANCH_W), lambda i: (jnp.maximum(i * r - 1, 0), 0)),
            pl.BlockSpec((tm, BRANCH_W), lambda i: (i, 0)),
            pl.BlockSpec((hl, BRANCH_W), lambda i: (jnp.minimum((i + 1) * r, n_halo - 1), 0)),
            _const_spec((CONV_W, SUBLANES, BRANCH_W)),
            _const_spec((1, BRANCH_W)), _const_spec((1, BRANCH_W)), _const_spec((1, BRANCH_W)),
        ],
        out_specs=pl.BlockSpec((tm, BRANCH_W), lambda i: (i, 0)),
        out_shape=jax.ShapeDtypeStruct((n_rows, BRANCH_W), BF16),
        scratch_shapes=[pltpu.VMEM((tm + 2 * hl, BRANCH_W), F32),
                        pltpu.VMEM((SUBLANES, tm + 2 * hl - SUBLANES, BRANCH_W), F32)],
        compiler_params=_cparams(("parallel",)),
        name="conv_module",
    )(y, y, y, jnp.broadcast_to(w[:, None, :], (CONV_W, SUBLANES, BRANCH_W)),
      vec(b), vec(g), vec(beta))


def _dft_kernel(u_ref, cc_ref, sc_ref, t_ref, o_ref, z_ref, *, seq):
    @pl.when(pl.program_id(1) == 0)
    def _():
        for gidx in range(D_GROUPS):
            ub = u_ref[:, gidx * LANES:(gidx + 1) * LANES]
            z_ref[0:seq, gidx * LANES:(gidx + 1) * LANES] = jnp.dot(
                ub, cc_ref[...], preferred_element_type=F32).astype(BF16)
            z_ref[seq:2 * seq, gidx * LANES:(gidx + 1) * LANES] = jnp.dot(
                ub, sc_ref[...], preferred_element_type=F32).astype(BF16)
    o_ref[...] = jnp.dot(t_ref[...], z_ref[...], preferred_element_type=F32).astype(BF16)


def _dft_tables(seq):
    def cs_rows(rows, n, period):
        ang = ((rows[:, None] * jnp.arange(n, dtype=jnp.int32)[None, :]) % period).astype(F32)
        ang = ang * (2.0 * math.pi / period)
        return jnp.cos(ang), jnp.sin(ang)

    def cs(n):
        split = 64
        if n <= split or n % split:
            return cs_rows(jnp.arange(n, dtype=jnp.int32), n, n)
        j = jnp.arange(split, dtype=jnp.int32)
        ac, as_ = cs_rows(jnp.arange(n // split, dtype=jnp.int32), n, n // split)
        bc, bs = cs_rows(j, n, n)
        cos = ac[:, None, :] * bc[None, :, :] - as_[:, None, :] * bs[None, :, :]
        sin = as_[:, None, :] * bc[None, :, :] + ac[:, None, :] * bs[None, :, :]
        return cos.reshape(n, n), sin.reshape(n, n)
    ct, st = cs(seq)
    cc, sc = cs(D_GROUP_CH)
    norm = 1.0 / math.sqrt(seq * D_GROUP_CH)
    tmat = jnp.concatenate([ct, -st], axis=1).astype(BF16)
    return (cc * norm).astype(BF16), (sc * norm).astype(BF16), tmat


def _fourier_mix(u, tables, n_batch, seq, row0, name):
    cc, sc, tmat = tables
    tr = min(DFT_ROW_TILE, seq)
    n_r = seq // tr
    return pl.pallas_call(
        functools.partial(_dft_kernel, seq=seq),
        grid=(n_batch, n_r),
        in_specs=[
            pl.BlockSpec((seq, BRANCH_W), lambda b, r: (row0 // seq + b, 0)),
            pl.BlockSpec((LANES, LANES), lambda b, r: (0, 0)),
            pl.BlockSpec((LANES, LANES), lambda b, r: (0, 0)),
            pl.BlockSpec((tr, 2 * seq), lambda b, r: (r, 0)),
        ],
        out_specs=pl.BlockSpec((tr, BRANCH_W), lambda b, r: (b * n_r + r, 0)),
        out_shape=jax.ShapeDtypeStruct((n_batch * seq, BRANCH_W), BF16),
        scratch_shapes=[pltpu.VMEM((2 * seq, BRANCH_W), BF16)],
        compiler_params=_cparams(("parallel", "arbitrary")),
        name=name,
    )(u, cc, sc, tmat)


def _merge_kernel(*refs, has_ctx, lat_tiles):
    refs = list(refs)
    n_y = sum(2 if c else 1 for c in has_ctx)
    y_refs, (gt_ref, h_ref, mod_ref, wba_ref, wb_ref, wo_ref, o_ref) = refs[:n_y], refs[n_y:]
    is_lat = pl.program_id(0) < lat_tiles
    acc = None
    for j, split in enumerate(has_ctx):
        y = y_refs.pop(0)[...]
        if split:
            y = jnp.where(is_lat, y, y_refs.pop(0)[...])
        w = wba_ref[...] if j == 0 else wb_ref[j]
        term = gt_ref[:, j * D_MODEL:(j + 1) * D_MODEL].astype(F32) * jnp.dot(
            y, w, preferred_element_type=F32)
        acc = term if acc is None else acc + term
    y = jnp.dot(acc.astype(BF16), wo_ref[...], preferred_element_type=F32)
    o_ref[...] = h_ref[...] + mod_ref[2:3, :] * y


def _merge(ys, gates, h, mods, layer, wb_a, wb, wo, n_tiles, lat_tiles, tiles_per_batch, ctx_row):
    tm = TOKEN_TILE
    row = lambda wd: pl.BlockSpec((tm, wd), lambda i: (i, 0))
    y_specs, y_args, has_ctx = [], [], []
    for y in ys:
        split = isinstance(y, tuple)
        has_ctx.append(split)
        if split:
            y_specs += [pl.BlockSpec((tm, BRANCH_W), lambda i: (jnp.minimum(i, lat_tiles - 1), 0)),
                        pl.BlockSpec((tm, BRANCH_W), lambda i: (jnp.maximum(i - lat_tiles, 0), 0))]
            y_args += list(y)
        else:
            y_specs.append(row(BRANCH_W))
            y_args.append(y)
    return pl.pallas_call(
        functools.partial(_merge_kernel, has_ctx=tuple(has_ctx), lat_tiles=lat_tiles),
        grid=(n_tiles,),
        in_specs=y_specs + [
            row(N_BRANCH * D_MODEL), row(D_MODEL),
            _mod_spec(layer, 1, lat_tiles, tiles_per_batch, ctx_row),
            _const_spec((BRANCH_W, D_MODEL)),
            _const_spec((N_BRANCH, BRANCH_W, D_MODEL), (layer,)),
            _const_spec((D_MODEL, D_MODEL), (layer,)),
        ],
        out_specs=row(D_MODEL),
        out_shape=jax.ShapeDtypeStruct((n_tiles * tm, D_MODEL), F32),
        compiler_params=_cparams(("parallel",)),
        name=f"merge_l{layer}",
    )(*y_args, gates, h, mods, wb_a, wb, wo)


def _rope_tables(seq, pad_rows):
    pos = jnp.arange(seq, dtype=jnp.int32)
    row = (pos // GRID_W).astype(F32)
    col = (pos % GRID_W).astype(F32)
    half = HEAD_DIM // 2
    inv = ROPE_BASE ** (-jnp.arange(0, half, 2, dtype=F32) / half)
    ang_r, ang_c = row[:, None] * inv, col[:, None] * inv
    ang = jnp.concatenate([ang_r, ang_r, ang_c, ang_c], axis=-1)
    cos, sin = jnp.cos(ang), jnp.sin(ang)
    lane = jnp.arange(HEAD_DIM)
    sin_s = jnp.where((lane & 16) == 0, -sin, sin)
    cos = jnp.concatenate([cos, jnp.ones((pad_rows, HEAD_DIM), F32)], axis=0)
    sin_s = jnp.concatenate([sin_s, jnp.zeros((pad_rows, HEAD_DIM), F32)], axis=0)
    return jnp.tile(cos, (1, 2)), jnp.tile(sin_s, (1, 2))


def _paired_query_weight(w_in):
    aq = w_in[:, :BRANCH_W].reshape(D_MODEL, A_KV_HEADS, A_GROUP, HEAD_DIM)
    return jnp.transpose(aq, (0, 2, 1, 3)).reshape(D_MODEL, BRANCH_W).astype(BF16)


def kernel(x, c, ctx, c_ctx, ada_w, ada_b, norm_g, ffn_w1, ffn_w3, ffn_w2, w_in, qk_norm_a,
           conv_w, conv_b, conv_ln_g, conv_ln_b, diff_lam, diff_subln_g, w_branch, w_out, final_g):
    n_batch, seq, _ = x.shape
    ctx_len = ctx.shape[1]
    depth = ada_w.shape[0]
    tm = TOKEN_TILE
    lat_rows, ctx_rows = n_batch * seq, n_batch * ctx_len
    assert seq % tm == 0 and ctx_rows % tm == 0 and ctx_len % CONV_TILE == 0
    assert n_batch < COND_ROWS and seq % GRID_W == 0
    lat_tiles, all_tiles = lat_rows // tm, (lat_rows + ctx_rows) // tm
    tiles_per_batch = seq // tm
    ctx_row = n_batch

    cond = jnp.concatenate(
        [c, c_ctx[None, :], jnp.zeros((COND_ROWS - n_batch - 1, D_MODEL), F32)], axis=0)
    mods = _adaln(cond, ada_w, ada_b).reshape(depth, COND_ROWS, 3, 3, D_MODEL)

    cos_t, sin_t = _rope_tables(seq, tm)
    lane = jnp.arange(LANES)
    seg = (lane[:, None] // HEAD_DIM == lane[None, :] // HEAD_DIM).astype(BF16)
    dft_lat = _dft_tables(seq)
    dft_ctx = _dft_tables(ctx_len)
    tile_args = (lat_tiles, tiles_per_batch, ctx_row)

    h, h_ctx = x.reshape(lat_rows, D_MODEL), ctx.reshape(ctx_rows, D_MODEL)
    w1_bf, w3_bf, w2_bf = ffn_w1.astype(BF16), ffn_w3.astype(BF16), ffn_w2.astype(BF16)
    w_in_bf, wb_bf, wo_bf = w_in.astype(BF16), w_branch.astype(BF16), w_out.astype(BF16)
    for l in range(depth):
        update_ctx = l < depth - 1
        lam_init = 0.8 - 0.6 * math.exp(-0.3 * l)
        out_tiles = all_tiles if update_ctx else lat_tiles
        out_rows = out_tiles * tm
        wb_a = wb_bf[l, 0].reshape(A_KV_HEADS, A_GROUP, HEAD_DIM, D_MODEL)
        wb_a = jnp.transpose(wb_a, (1, 0, 2, 3)).reshape(BRANCH_W, D_MODEL)
        qkg = jnp.tile(qk_norm_a[l], (1, 2))

        h = _half_ffn(h, mods, l, 0, norm_g[l, 0], w1_bf, w3_bf, w2_bf, (l, 0),
                      all_tiles, *tile_args, h_ctx=h_ctx)
        h_ctx = None
        qa, ka, va, yg, qc, kc, vc, du, gates = _in_proj(
            h, mods, l, norm_g[l, 1], w_in_bf, _paired_query_weight(w_in[l]), cos_t, sin_t,
            seg, qkg, all_tiles, *tile_args)

        lat_src = ((0, seq), (lat_rows, ctx_len))
        ctx_src = ((lat_rows, ctx_len),)
        diff_extra = (diff_lam[l], diff_subln_g[l].reshape(1, LANES))
        y_a = _attention("gqa", qa, ka, va, None, n_batch, seq, 0, lat_src, name=f"gqa_l{l}")
        y_c = _attention("diff", qc, kc, vc, diff_extra, n_batch, seq, 0, lat_src,
                         lam_init=lam_init, name=f"diff_l{l}")
        y_b = _conv_module(yg, conv_w[l], conv_b[l], conv_ln_g[l], conv_ln_b[l],
                           out_rows, lat_rows, seq, ctx_len)
        y_d = _fourier_mix(du, dft_lat, n_batch, seq, 0, f"dft_l{l}")
        if update_ctx:
            y_a = (y_a, _attention("gqa", qa, ka, va, None, n_batch, ctx_len, lat_rows, ctx_src,
                                   name=f"gqa_ctx_l{l}"))
            y_c = (y_c, _attention("diff", qc, kc, vc, diff_extra, n_batch, ctx_len, lat_rows,
                                   ctx_src, lam_init=lam_init, name=f"diff_ctx_l{l}"))
            y_d = (y_d, _fourier_mix(du, dft_ctx, n_batch, ctx_len, lat_rows, f"dft_ctx_l{l}"))
        h = _merge((y_a, y_b, y_c, y_d), gates, h, mods, l, wb_a, wb_bf, wo_bf,
                   out_tiles, *tile_args)
        h = _half_ffn(h, mods, l, 2, norm_g[l, 2], w1_bf, w3_bf, w2_bf, (l, 1),
                      out_tiles, *tile_args, final_g=None if update_ctx else final_g)
    return h.reshape(n_batch, seq, D_MODEL)
```

```python
import functools
import math

import jax
import jax.numpy as jnp
from jax import lax
from jax.experimental import pallas as pl
from jax.experimental.pallas import tpu as pltpu

F32 = jnp.float32
BF16 = jnp.bfloat16

D_MODEL = 1024
GRID_W = 64
HEAD_DIM = 64
ROPE_BASE = 10000.0
EPS = 1e-6
N_MOD = 9
N_BRANCH = 4
BRANCH_W = D_MODEL // 2
A_Q_HEADS = BRANCH_W // HEAD_DIM
A_KV_HEADS = 2
A_GROUP = A_Q_HEADS // A_KV_HEADS
CONV_W = 31
C_HEADS = BRANCH_W // (2 * HEAD_DIM)
D_GROUPS = 4
D_GROUP_CH = BRANCH_W // D_GROUPS
D_FF = 256 * ((8 * D_MODEL // 3 + 255) // 256)
ATTN_SCALE = HEAD_DIM ** -0.5

LANES = 128
SUBLANES = 8
COND_ROWS = 16
TOKEN_TILE = 512
FF_CHUNK = 256
CONV_TILE = 256
CONV_HALO = 16
CONV_ROWS = 32
ATTN_Q_ROWS = 1024
ATTN_KEY_CHUNK = 1024
Q_SCALE = ATTN_SCALE * math.log2(math.e)
DFT_ROW_TILE = 512
VMEM_LIMIT = 56 * 1024 * 1024


def _cparams(sem):
    return pltpu.CompilerParams(dimension_semantics=sem, vmem_limit_bytes=VMEM_LIMIT)


def _sigmoid(x):
    return 1.0 / (1.0 + jnp.exp(-x))


def _const_spec(shape, lead=()):
    nd = len(shape)
    return pl.BlockSpec((None,) * len(lead) + tuple(shape), lambda *_: tuple(lead) + (0,) * nd,
                        pipeline_mode=pl.Buffered(1))


def _ada_kernel(cond_ref, w_ref, b_ref, o_ref):
    cnd = cond_ref[...]
    s = cnd * _sigmoid(cnd)
    o_ref[...] = jnp.dot(s, w_ref[...], preferred_element_type=F32,
                         precision=lax.Precision.HIGHEST) + b_ref[...]


def _adaln(cond, ada_w, ada_b):
    depth = ada_w.shape[0]
    n_out = N_MOD * D_MODEL
    col = D_MODEL
    return pl.pallas_call(
        _ada_kernel,
        grid=(depth, n_out // col),
        in_specs=[
            pl.BlockSpec((COND_ROWS, D_MODEL), lambda l, j: (0, 0)),
            pl.BlockSpec((None, D_MODEL, col), lambda l, j: (l, 0, j)),
            pl.BlockSpec((None, 1, col), lambda l, j: (l, 0, j)),
        ],
        out_specs=pl.BlockSpec((None, COND_ROWS, col), lambda l, j: (l, 0, j)),
        out_shape=jax.ShapeDtypeStruct((depth, COND_ROWS, n_out), F32),
        compiler_params=_cparams(("arbitrary", "arbitrary")),
        name="adaln",
    )(cond, ada_w, ada_b.reshape(depth, 1, n_out))


def _rms(x, g):
    return x * lax.rsqrt(jnp.mean(x * x, axis=-1, keepdims=True) + EPS) * g


def _mod_spec(layer, sub, lat_tiles, tiles_per_batch, ctx_row):
    def idx(i):
        row = jnp.where(i < lat_tiles, i // tiles_per_batch, ctx_row)
        return (layer, row, sub, 0, 0)
    return pl.BlockSpec((None, None, None, 3, D_MODEL), idx)


def _ffn_kernel(*refs, final, lat_tiles, split):
    refs = list(refs)
    h_ref = refs.pop(0)
    ctx_ref = refs.pop(0) if split else None
    mod_ref, g_ref, w1_ref, w3_ref, w2_ref = refs[:5]
    fg_ref = refs[5] if final else None
    o_ref = refs[-1]
    x = h_ref[...]
    if split:
        x = jnp.where(pl.program_id(0) < lat_tiles, x, ctx_ref[...])
    shift, scale, gate = mod_ref[0:1, :], mod_ref[1:2, :], mod_ref[2:3, :]
    n = (_rms(x, g_ref[...]) * (1.0 + scale) + shift).astype(BF16)
    acc = jnp.zeros(x.shape, F32)
    c = FF_CHUNK
    for j in range(D_FF // c):
        a = jnp.dot(n, w1_ref[:, c * j:c * (j + 1)], preferred_element_type=F32)
        b = jnp.dot(n, w3_ref[:, c * j:c * (j + 1)], preferred_element_type=F32)
        p = (a * _sigmoid(a) * b).astype(BF16)
        acc = acc + jnp.dot(p, w2_ref[c * j:c * (j + 1), :], preferred_element_type=F32)
    out = x + 0.5 * gate * acc
    if final:
        out = _rms(out, fg_ref[...])
    o_ref[...] = out


def _half_ffn(h, mods, layer, sub, g, w1, w3, w2, wsel, n_tiles, lat_tiles, tiles_per_batch, ctx_row,
              final_g=None, h_ctx=None):
    tm = TOKEN_TILE
    final = final_g is not None
    split = h_ctx is not None
    if split:
        in_specs = [pl.BlockSpec((tm, D_MODEL), lambda i: (jnp.minimum(i, lat_tiles - 1), 0)),
                    pl.BlockSpec((tm, D_MODEL), lambda i: (jnp.maximum(i - lat_tiles, 0), 0))]
        args = [h, h_ctx]
    else:
        in_specs = [pl.BlockSpec((tm, D_MODEL), lambda i: (i, 0))]
        args = [h]
    in_specs += [
        _mod_spec(layer, sub, lat_tiles, tiles_per_batch, ctx_row),
        _const_spec((1, D_MODEL)),
        _const_spec((D_MODEL, D_FF), wsel),
        _const_spec((D_MODEL, D_FF), wsel),
        _const_spec((D_FF, D_MODEL), wsel),
    ]
    args += [mods, g.reshape(1, D_MODEL), w1, w3, w2]
    if final:
        in_specs.append(_const_spec((1, D_MODEL)))
        args.append(final_g.reshape(1, D_MODEL))
    return pl.pallas_call(
        functools.partial(_ffn_kernel, final=final, lat_tiles=lat_tiles, split=split),
        grid=(n_tiles,),
        in_specs=in_specs,
        out_specs=pl.BlockSpec((tm, D_MODEL), lambda i: (i, 0)),
        out_shape=jax.ShapeDtypeStruct((n_tiles * tm, D_MODEL), F32),
        compiler_params=_cparams(("parallel",)),
        name=f"ffn_l{layer}_s{sub}",
    )(*args)


_IN_SEGS = (("aq", 512), ("ak", 128), ("av", 128), ("ba", 512), ("bg", 512),
            ("cq", 512), ("ck", 512), ("cv", 512), ("du", 512), ("gl", 4096))
_IN_OFF = {}
_o = 0
for _n, _s in _IN_SEGS:
    _IN_OFF[_n] = (_o, _o + _s)
    _o += _s
IN_COLS = _o


def _rope_blocks(x, cos, sin_s, low16):
    outs = []
    for j in range(x.shape[1] // LANES):
        xb = x[:, j * LANES:(j + 1) * LANES]
        partner = jnp.where(low16, pltpu.roll(xb, LANES - 16, 1), pltpu.roll(xb, 16, 1))
        outs.append(xb * cos + partner * sin_s)
    return outs


def _in_kernel(h_ref, mod_ref, g_ref, w_ref, wq_ref, cos_ref, sin_ref, seg_ref, qkg_ref,
               qa_ref, ka_ref, va_ref, yg_ref, qc_ref, kc_ref, vc_ref, du_ref, gt_ref):
    x = h_ref[...]
    shift, scale = mod_ref[0:1, :], mod_ref[1:2, :]
    n = (_rms(x, g_ref[...]) * (1.0 + scale) + shift).astype(BF16)

    def proj(name, lo=None, hi=None):
        if name == "aq":
            return jnp.dot(n, wq_ref[...], preferred_element_type=F32)
        a, b = _IN_OFF[name]
        if lo is not None:
            a, b = a + lo, a + hi
        return jnp.dot(n, w_ref[:, a:b], preferred_element_type=F32)

    cos, sin_s = cos_ref[...], sin_ref[...]
    lane = lax.broadcasted_iota(jnp.int32, (1, LANES), 1)
    low16 = (lane & 16) == 0
    seg = seg_ref[...]

    def head_norm(z, g):
        outs = []
        for j in range(z.shape[1] // LANES):
            zb = z[:, j * LANES:(j + 1) * LANES]
            ss = jnp.dot((zb * zb).astype(BF16), seg, preferred_element_type=F32)
            outs.append(zb * lax.rsqrt(ss * (1.0 / HEAD_DIM) + EPS) * g)
        return jnp.concatenate(outs, axis=1)

    q = head_norm(proj("aq"), qkg_ref[0:1, :])
    qa_ref[...] = jnp.concatenate(
        [b * Q_SCALE for b in _rope_blocks(q, cos, sin_s, low16)], axis=1).astype(BF16)
    k = head_norm(proj("ak"), qkg_ref[1:2, :])
    ka_ref[...] = _rope_blocks(k, cos, sin_s, low16)[0].astype(BF16)
    va_ref[...] = proj("av").astype(BF16)
    yg_ref[...] = proj("ba") * _sigmoid(proj("bg"))
    qc_ref[...] = jnp.concatenate(
        [b * Q_SCALE for b in _rope_blocks(proj("cq"), cos, sin_s, low16)], axis=1).astype(BF16)
    kc_ref[...] = jnp.concatenate(_rope_blocks(proj("ck"), cos, sin_s, low16), axis=1).astype(BF16)
    vc_ref[...] = proj("cv").astype(BF16)
    du_ref[...] = proj("du").astype(BF16)
    for j in range(N_BRANCH):
        gt_ref[:, j * D_MODEL:(j + 1) * D_MODEL] = _sigmoid(
            proj("gl", j * D_MODEL, (j + 1) * D_MODEL)).astype(BF16)


def _in_proj(h, mods, layer, g, w, wq, cos_t, sin_t, seg, qkg, n_tiles, lat_tiles,
             tiles_per_batch, ctx_row):
    tm = TOKEN_TILE
    t = n_tiles * tm

    def rope_idx(i):
        return (jnp.where(i < lat_tiles, i % tiles_per_batch, tiles_per_batch), 0)

    widths = (512, 128, 128, 512, 512, 512, 512, 512, 4096)
    dtypes = (BF16, BF16, BF16, F32, BF16, BF16, BF16, BF16, BF16)
    return pl.pallas_call(
        _in_kernel,
        grid=(n_tiles,),
        in_specs=[
            pl.BlockSpec((tm, D_MODEL), lambda i: (i, 0)),
            _mod_spec(layer, 1, lat_tiles, tiles_per_batch, ctx_row),
            _const_spec((1, D_MODEL)),
            _const_spec((D_MODEL, IN_COLS), (layer,)),
            _const_spec((D_MODEL, BRANCH_W)),
            pl.BlockSpec((tm, LANES), rope_idx),
            pl.BlockSpec((tm, LANES), rope_idx),
            _const_spec((LANES, LANES)),
            _const_spec((2, LANES)),
        ],
        out_specs=[pl.BlockSpec((tm, wd), lambda i: (i, 0)) for wd in widths],
        out_shape=[jax.ShapeDtypeStruct((t, wd), dt) for wd, dt in zip(widths, dtypes)],
        compiler_params=_cparams(("parallel",)),
        name=f"in_proj_l{layer}",
    )(h, mods, g.reshape(1, D_MODEL), w, wq, cos_t, sin_t, seg, qkg)


def _attn_kernel(*refs, kind, key_counts, n_units, lam_init):
    n_src = len(key_counts)
    q_ref = refs[0]
    k_refs, v_refs = refs[1:1 + n_src], refs[1 + n_src:1 + 2 * n_src]
    rest = refs[1 + 2 * n_src:]
    if kind == "gqa":
        o_ref, s_a, s_b, mx_a, mx_b = rest
    else:
        lam_ref, g_ref, o_ref, s_a, s_b, mx_a, mx_b, o0_scr = rest
    t = pl.program_id(0)

    @pl.when(t == 0)
    def _():
        s_b[...] = jnp.zeros(s_b.shape, F32)
        mx_b[...] = jnp.zeros(mx_b.shape, F32)
        if kind == "gqa":
            o_ref[...] = jnp.zeros(o_ref.shape, BF16)
        else:
            o0_scr[...] = jnp.zeros(o0_scr.shape, F32)

    def step(front_half, s_front, mx_front, s_back, mx_back):
        lane_half = lax.broadcasted_iota(jnp.int32, (1, LANES), 1) // HEAD_DIM
        qp = jnp.where(lane_half == front_half, q_ref[...].astype(F32), 0.0).astype(BF16)
        m_back = mx_back[:, 0:1]
        run_max, acc, off = None, None, 0
        for k_ref, v_ref, n_keys in zip(k_refs, v_refs, key_counts):
            for c0 in range(0, n_keys, ATTN_KEY_CHUNK):
                kc = min(ATTN_KEY_CHUNK, n_keys - c0)
                s = lax.dot_general(qp, k_ref[c0:c0 + kc, :], (((1,), (1,)), ((), ())),
                                    preferred_element_type=F32)
                s_front[:, off:off + kc] = s
                part = functools.reduce(
                    jnp.maximum, [s[:, j * LANES:(j + 1) * LANES] for j in range(kc // LANES)])
                run_max = part if run_max is None else jnp.maximum(run_max, part)
                p = jnp.exp2(s_back[:, off:off + kc] - m_back).astype(BF16)
                v_ext = jnp.concatenate(
                    [v_ref[c0:c0 + kc, :], jnp.ones((kc, LANES), BF16)], axis=1)
                pv = jnp.dot(p, v_ext, preferred_element_type=F32)
                acc = pv if acc is None else acc + pv
                off += kc
        mx_front[...] = jnp.broadcast_to(jnp.max(run_max, axis=-1, keepdims=True), run_max.shape)
        o = acc[:, :LANES] / acc[:, LANES:]
        back_half = 1 - front_half
        if kind == "gqa":
            if back_half == 0:
                o_ref[...] = o.astype(BF16)
            else:
                low = lax.broadcasted_iota(jnp.int32, (1, LANES), 1) < HEAD_DIM
                o_ref[...] = jnp.where(low, o_ref[...], o.astype(BF16))
        elif back_half == 0:
            o0_scr[...] = o
        else:
            lp = lam_ref[...]
            lam = (jnp.exp(jnp.sum(lp[0:1, :] * lp[1:2, :], axis=-1, keepdims=True))
                   - jnp.exp(jnp.sum(lp[2:3, :] * lp[3:4, :], axis=-1, keepdims=True)) + lam_init)
            o_ref[...] = (_rms(o0_scr[...] - lam * o, g_ref[...]) * (1.0 - lam_init)).astype(BF16)

    @pl.when(t % 2 == 0)
    def _():
        step(0, s_a, mx_a, s_b, mx_b)

    @pl.when(t % 2 == 1)
    def _():
        step(1, s_b, mx_b, s_a, mx_a)


def _attention(kind, q, k, v, extra, n_batch, q_rows, q_row0, sources, lam_init=None, name=""):
    tq = min(ATTN_Q_ROWS, q_rows)
    n_q = q_rows // tq
    per_tile = 2 * BRANCH_W // LANES
    n_units = n_batch * n_q * per_tile
    kv_col = (lambda blk: 0) if kind == "gqa" else (lambda blk: blk)

    def decode(x):
        return x // (n_q * per_tile), (x // per_tile) % n_q, (x % per_tile) // 2

    front = lambda t: decode(jnp.minimum(t, n_units - 1))
    back = lambda t: decode(jnp.maximum(t - 1, 0))

    def q_idx(t):
        b, i, blk = front(t)
        return (q_row0 // tq + b * n_q + i, blk)

    def o_idx(t):
        b, i, blk = back(t)
        return (b * n_q + i, blk)

    def kv_idx(t, which, r):
        b, _, blk = which(t)
        return (r + b, kv_col(blk))

    in_specs = [pl.BlockSpec((tq, LANES), q_idx)]
    args = [q]
    for arr, which in ((k, front), (v, back)):
        for row0, n_keys in sources:
            in_specs.append(pl.BlockSpec(
                (n_keys, LANES), functools.partial(kv_idx, which=which, r=row0 // n_keys)))
            args.append(arr)
    total_keys = sum(nk for _, nk in sources)
    scratch = [pltpu.VMEM((tq, total_keys), F32), pltpu.VMEM((tq, total_keys), F32),
               pltpu.VMEM((tq, LANES), F32), pltpu.VMEM((tq, LANES), F32)]
    if kind == "diff":
        in_specs += [pl.BlockSpec(extra[0].shape, lambda t: (0, 0)),
                     pl.BlockSpec(extra[1].shape, lambda t: (0, 0))]
        args += list(extra)
        scratch.append(pltpu.VMEM((tq, LANES), F32))
    return pl.pallas_call(
        functools.partial(_attn_kernel, kind=kind, key_counts=tuple(nk for _, nk in sources),
                          n_units=n_units, lam_init=lam_init),
        grid=(n_units + 1,),
        in_specs=in_specs,
        out_specs=pl.BlockSpec((tq, LANES), o_idx),
        out_shape=jax.ShapeDtypeStruct((n_batch * q_rows, BRANCH_W), BF16),
        scratch_shapes=scratch,
        compiler_params=_cparams(("arbitrary",)),
        name=name,
    )(*args)


def _conv_kernel(prev_ref, cur_ref, next_ref, w_ref, b_ref, g_ref, beta_ref, o_ref, win_ref,
                 stage_ref,
                 *, lat_tiles, lat_per_seq, ctx_per_seq):
    i = pl.program_id(0)
    j = jnp.where(i < lat_tiles, i % lat_per_seq, (i - lat_tiles) % ctx_per_seq)
    per = jnp.where(i < lat_tiles, lat_per_seq, ctx_per_seq)
    first, last = j == 0, j == per - 1
    hl, tm = CONV_HALO, CONV_TILE
    win_ref[0:hl, :] = jnp.where(first, 0.0, prev_ref[...])
    win_ref[hl:hl + tm, :] = cur_ref[...]
    win_ref[hl + tm:hl + tm + hl, :] = jnp.where(last, 0.0, next_ref[...])
    half = CONV_W // 2
    rc = CONV_ROWS
    span = stage_ref.shape[1]
    for phase in range(SUBLANES):
        stage_ref[phase] = win_ref[phase:phase + span, :]
    for c in range(tm // rc):
        acc = jnp.zeros((rc // SUBLANES, SUBLANES, BRANCH_W), F32)
        for k in range(CONV_W):
            off = k + hl - half
            r0 = c * rc + off - off % SUBLANES
            xs = stage_ref[off % SUBLANES, r0:r0 + rc, :]
            acc = acc + xs.reshape(rc // SUBLANES, SUBLANES, BRANCH_W) * w_ref[k]
        y = acc.reshape(rc, BRANCH_W) + b_ref[...]
        mu = jnp.mean(y, axis=-1, keepdims=True)
        yc = y - mu
        var = jnp.mean(yc * yc, axis=-1, keepdims=True)
        yn = yc * lax.rsqrt(var + EPS) * g_ref[...] + beta_ref[...]
        o_ref[c * rc:(c + 1) * rc, :] = (yn * _sigmoid(yn)).astype(BF16)


def _conv_module(y, w, b, g, beta, n_rows, lat_rows, seq, ctx_len):
    tm, hl = CONV_TILE, CONV_HALO
    n_tiles = n_rows // tm
    r = tm // hl
    n_halo = y.shape[0] // hl
    body = functools.partial(_conv_kernel, lat_tiles=lat_rows // tm, lat_per_seq=seq // tm,
                             ctx_per_seq=max(ctx_len // tm, 1))
    vec = lambda a: a.reshape(1, BRANCH_W)
    return pl.pallas_call(
        body,
        grid=(n_tiles,),
        in_specs=[
            pl.BlockSpec((hl, BRANCH_W), lambda i: (jnp.maximum(i * r - 1, 0), 0)),
            pl.BlockSpec((tm, BRANCH_W), lambda i: (i, 0)),
            pl.BlockSpec((hl, BRANCH_W), lambda i: (jnp.minimum((i + 1) * r, n_halo - 1), 0)),
            _const_spec((CONV_W, SUBLANES, BRANCH_W)),
            _const_spec((1, BRANCH_W)), _const_spec((1, BRANCH_W)), _const_spec((1, BRANCH_W)),
        ],
        out_specs=pl.BlockSpec((tm, BRANCH_W), lambda i: (i, 0)),
        out_shape=jax.ShapeDtypeStruct((n_rows, BRANCH_W), BF16),
        scratch_shapes=[pltpu.VMEM((tm + 2 * hl, BRANCH_W), F32),
                        pltpu.VMEM((SUBLANES, tm + 2 * hl - SUBLANES, BRANCH_W), F32)],
        compiler_params=_cparams(("parallel",)),
        name="conv_module",
    )(y, y, y, jnp.broadcast_to(w[:, None, :], (CONV_W, SUBLANES, BRANCH_W)),
      vec(b), vec(g), vec(beta))


def _dft_kernel(u_ref, cc_ref, sc_ref, flip_ref, t_ref, o_ref, z_ref, mid_ref, *, seq, tile):
    half = seq // 2
    n_fold = half // tile

    @pl.when(pl.program_id(1) == 0)
    def _():
        row = lax.broadcasted_iota(jnp.int32, (tile, 1), 0)
        for j in range(n_fold):
            src = (2 * n_fold - 1 - j) * tile
            rev = jnp.dot(flip_ref[...], u_ref[src:src + tile, :], preferred_element_type=F32)
            if j > 0:
                rev = jnp.where(row == 0, u_ref[src + tile:src + tile + 1, :].astype(F32), rev)
            x = u_ref[j * tile:(j + 1) * tile, :].astype(F32)
            plus, minus = (x + rev).astype(BF16), (x - rev).astype(BF16)
            for gidx in range(D_GROUPS):
                cols = slice(gidx * LANES, (gidx + 1) * LANES)
                z_ref[j * tile:(j + 1) * tile, cols] = jnp.dot(
                    plus[:, cols], cc_ref[...], preferred_element_type=F32).astype(BF16)
                z_ref[half + j * tile:half + (j + 1) * tile, cols] = jnp.dot(
                    minus[:, cols], sc_ref[...], preferred_element_type=F32).astype(BF16)
        for gidx in range(D_GROUPS):
            cols = slice(gidx * LANES, (gidx + 1) * LANES)
            mid_ref[:, cols] = jnp.dot(u_ref[half:half + mid_ref.shape[0], cols], cc_ref[...],
                                       preferred_element_type=F32)

    k_odd = lax.broadcasted_iota(jnp.int32, (tile, 1), 0) % 2
    sign = 1.0 - 2.0 * k_odd.astype(F32)
    o_ref[...] = (jnp.dot(t_ref[...], z_ref[...], preferred_element_type=F32)
                  + sign * mid_ref[0:1, :]).astype(BF16)


def _dft_tables(seq):
    def cs_rows(rows, n, period):
        ang = ((rows[:, None] * jnp.arange(n, dtype=jnp.int32)[None, :]) % period).astype(F32)
        ang = ang * (2.0 * math.pi / period)
        return jnp.cos(ang), jnp.sin(ang)

    def cs(n):
        split = 64
        if n <= split or n % split:
            return cs_rows(jnp.arange(n, dtype=jnp.int32), n, n)
        j = jnp.arange(split, dtype=jnp.int32)
        ac, as_ = cs_rows(jnp.arange(n // split, dtype=jnp.int32), n, n // split)
        bc, bs = cs_rows(j, n, n)
        cos = ac[:, None, :] * bc[None, :, :] - as_[:, None, :] * bs[None, :, :]
        sin = as_[:, None, :] * bc[None, :, :] + ac[:, None, :] * bs[None, :, :]
        return cos.reshape(n, n), sin.reshape(n, n)
    ct, st = cs(seq)
    cc, sc = cs(D_GROUP_CH)
    norm = 1.0 / math.sqrt(seq * D_GROUP_CH)
    half = seq // 2
    tmat = jnp.concatenate([ct[:, :half], -st[:, :half]], axis=1).astype(BF16)
    tile = min(DFT_ROW_TILE, half)
    i = jnp.arange(tile, dtype=jnp.int32)
    flip = ((i[:, None] + i[None, :] == tile) & (i[:, None] > 0)).astype(BF16)
    return (cc * norm).astype(BF16), (sc * norm).astype(BF16), flip, tmat


def _fourier_mix(u, tables, n_batch, seq, row0, name):
    cc, sc, flip, tmat = tables
    tr = flip.shape[0]
    n_r = seq // tr
    mid_rows = 2 * SUBLANES
    return pl.pallas_call(
        functools.partial(_dft_kernel, seq=seq, tile=tr),
        grid=(n_batch, n_r),
        in_specs=[
            pl.BlockSpec((seq, BRANCH_W), lambda b, r: (row0 // seq + b, 0)),
            pl.BlockSpec((LANES, LANES), lambda b, r: (0, 0)),
            pl.BlockSpec((LANES, LANES), lambda b, r: (0, 0)),
            pl.BlockSpec((tr, tr), lambda b, r: (0, 0)),
            pl.BlockSpec((tr, seq), lambda b, r: (r, 0)),
        ],
        out_specs=pl.BlockSpec((tr, BRANCH_W), lambda b, r: (b * n_r + r, 0)),
        out_shape=jax.ShapeDtypeStruct((n_batch * seq, BRANCH_W), BF16),
        scratch_shapes=[pltpu.VMEM((seq, BRANCH_W), BF16), pltpu.VMEM((mid_rows, BRANCH_W), F32)],
        compiler_params=_cparams(("parallel", "arbitrary")),
        name=name,
    )(u, cc, sc, flip, tmat)


def _merge_kernel(*refs, has_ctx, lat_tiles):
    refs = list(refs)
    n_y = sum(2 if c else 1 for c in has_ctx)
    y_refs, (gt_ref, h_ref, mod_ref, wba_ref, wb_ref, wo_ref, o_ref) = refs[:n_y], refs[n_y:]
    is_lat = pl.program_id(0) < lat_tiles
    acc = None
    for j, split in enumerate(has_ctx):
        y = y_refs.pop(0)[...]
        if split:
            y = jnp.where(is_lat, y, y_refs.pop(0)[...])
        w = wba_ref[...] if j == 0 else wb_ref[j]
        term = gt_ref[:, j * D_MODEL:(j + 1) * D_MODEL].astype(F32) * jnp.dot(
            y, w, preferred_element_type=F32)
        acc = term if acc is None else acc + term
    y = jnp.dot(acc.astype(BF16), wo_ref[...], preferred_element_type=F32)
    o_ref[...] = h_ref[...] + mod_ref[2:3, :] * y


def _merge(ys, gates, h, mods, layer, wb_a, wb, wo, n_tiles, lat_tiles, tiles_per_batch, ctx_row):
    tm = TOKEN_TILE
    row = lambda wd: pl.BlockSpec((tm, wd), lambda i: (i, 0))
    y_specs, y_args, has_ctx = [], [], []
    for y in ys:
        split = isinstance(y, tuple)
        has_ctx.append(split)
        if split:
            y_specs += [pl.BlockSpec((tm, BRANCH_W), lambda i: (jnp.minimum(i, lat_tiles - 1), 0)),
                        pl.BlockSpec((tm, BRANCH_W), lambda i: (jnp.maximum(i - lat_tiles, 0), 0))]
            y_args += list(y)
        else:
            y_specs.append(row(BRANCH_W))
            y_args.append(y)
    return pl.pallas_call(
        functools.partial(_merge_kernel, has_ctx=tuple(has_ctx), lat_tiles=lat_tiles),
        grid=(n_tiles,),
        in_specs=y_specs + [
            row(N_BRANCH * D_MODEL), row(D_MODEL),
            _mod_spec(layer, 1, lat_tiles, tiles_per_batch, ctx_row),
            _const_spec((BRANCH_W, D_MODEL)),
            _const_spec((N_BRANCH, BRANCH_W, D_MODEL), (layer,)),
            _const_spec((D_MODEL, D_MODEL), (layer,)),
        ],
        out_specs=row(D_MODEL),
        out_shape=jax.ShapeDtypeStruct((n_tiles * tm, D_MODEL), F32),
        compiler_params=_cparams(("parallel",)),
        name=f"merge_l{layer}",
    )(*y_args, gates, h, mods, wb_a, wb, wo)


def _rope_tables(seq, pad_rows):
    pos = jnp.arange(seq, dtype=jnp.int32)
    row = (pos // GRID_W).astype(F32)
    col = (pos % GRID_W).astype(F32)
    half = HEAD_DIM // 2
    inv = ROPE_BASE ** (-jnp.arange(0, half, 2, dtype=F32) / half)
    ang_r, ang_c = row[:, None] * inv, col[:, None] * inv
    ang = jnp.concatenate([ang_r, ang_r, ang_c, ang_c], axis=-1)
    cos, sin = jnp.cos(ang), jnp.sin(ang)
    lane = jnp.arange(HEAD_DIM)
    sin_s = jnp.where((lane & 16) == 0, -sin, sin)
    cos = jnp.concatenate([cos, jnp.ones((pad_rows, HEAD_DIM), F32)], axis=0)
    sin_s = jnp.concatenate([sin_s, jnp.zeros((pad_rows, HEAD_DIM), F32)], axis=0)
    return jnp.tile(cos, (1, 2)), jnp.tile(sin_s, (1, 2))


def _paired_query_weight(w_in):
    aq = w_in[:, :BRANCH_W].reshape(D_MODEL, A_KV_HEADS, A_GROUP, HEAD_DIM)
    return jnp.transpose(aq, (0, 2, 1, 3)).reshape(D_MODEL, BRANCH_W).astype(BF16)


def kernel(x, c, ctx, c_ctx, ada_w, ada_b, norm_g, ffn_w1, ffn_w3, ffn_w2, w_in, qk_norm_a,
           conv_w, conv_b, conv_ln_g, conv_ln_b, diff_lam, diff_subln_g, w_branch, w_out, final_g):
    n_batch, seq, _ = x.shape
    ctx_len = ctx.shape[1]
    depth = ada_w.shape[0]
    tm = TOKEN_TILE
    lat_rows, ctx_rows = n_batch * seq, n_batch * ctx_len
    assert seq % tm == 0 and ctx_rows % tm == 0 and ctx_len % CONV_TILE == 0
    assert n_batch < COND_ROWS and seq % GRID_W == 0
    lat_tiles, all_tiles = lat_rows // tm, (lat_rows + ctx_rows) // tm
    tiles_per_batch = seq // tm
    ctx_row = n_batch

    cond = jnp.concatenate(
        [c, c_ctx[None, :], jnp.zeros((COND_ROWS - n_batch - 1, D_MODEL), F32)], axis=0)
    mods = _adaln(cond, ada_w, ada_b).reshape(depth, COND_ROWS, 3, 3, D_MODEL)

    cos_t, sin_t = _rope_tables(seq, tm)
    lane = jnp.arange(LANES)
    seg = (lane[:, None] // HEAD_DIM == lane[None, :] // HEAD_DIM).astype(BF16)
    dft_lat = _dft_tables(seq)
    dft_ctx = _dft_tables(ctx_len)
    tile_args = (lat_tiles, tiles_per_batch, ctx_row)

    h, h_ctx = x.reshape(lat_rows, D_MODEL), ctx.reshape(ctx_rows, D_MODEL)
    w1_bf, w3_bf, w2_bf = ffn_w1.astype(BF16), ffn_w3.astype(BF16), ffn_w2.astype(BF16)
    w_in_bf, wb_bf, wo_bf = w_in.astype(BF16), w_branch.astype(BF16), w_out.astype(BF16)
    for l in range(depth):
        update_ctx = l < depth - 1
        lam_init = 0.8 - 0.6 * math.exp(-0.3 * l)
        out_tiles = all_tiles if update_ctx else lat_tiles
        out_rows = out_tiles * tm
        wb_a = wb_bf[l, 0].reshape(A_KV_HEADS, A_GROUP, HEAD_DIM, D_MODEL)
        wb_a = jnp.transpose(wb_a, (1, 0, 2, 3)).reshape(BRANCH_W, D_MODEL)
        qkg = jnp.tile(qk_norm_a[l], (1, 2))

        h = _half_ffn(h, mods, l, 0, norm_g[l, 0], w1_bf, w3_bf, w2_bf, (l, 0),
                      all_tiles, *tile_args, h_ctx=h_ctx)
        h_ctx = None
        qa, ka, va, yg, qc, kc, vc, du, gates = _in_proj(
            h, mods, l, norm_g[l, 1], w_in_bf, _paired_query_weight(w_in[l]), cos_t, sin_t,
            seg, qkg, all_tiles, *tile_args)

        lat_src = ((0, seq), (lat_rows, ctx_len))
        ctx_src = ((lat_rows, ctx_len),)
        diff_extra = (diff_lam[l], diff_subln_g[l].reshape(1, LANES))
        y_a = _attention("gqa", qa, ka, va, None, n_batch, seq, 0, lat_src, name=f"gqa_l{l}")
        y_c = _attention("diff", qc, kc, vc, diff_extra, n_batch, seq, 0, lat_src,
                         lam_init=lam_init, name=f"diff_l{l}")
        y_b = _conv_module(yg, conv_w[l], conv_b[l], conv_ln_g[l], conv_ln_b[l],
                           out_rows, lat_rows, seq, ctx_len)
        y_d = _fourier_mix(du, dft_lat, n_batch, seq, 0, f"dft_l{l}")
        if update_ctx:
            y_a = (y_a, _attention("gqa", qa, ka, va, None, n_batch, ctx_len, lat_rows, ctx_src,
                                   name=f"gqa_ctx_l{l}"))
            y_c = (y_c, _attention("diff", qc, kc, vc, diff_extra, n_batch, ctx_len, lat_rows,
                                   ctx_src, lam_init=lam_init, name=f"diff_ctx_l{l}"))
            y_d = (y_d, _fourier_mix(du, dft_ctx, n_batch, ctx_len, lat_rows, f"dft_ctx_l{l}"))
        h = _merge((y_a, y_b, y_c, y_d), gates, h, mods, l, wb_a, wb_bf, wo_bf,
                   out_tiles, *tile_args)
        h = _half_ffn(h, mods, l, 2, norm_g[l, 2], w1_bf, w3_bf, w2_bf, (l, 1),
                      out_tiles, *tile_args, final_g=None if update_ctx else final_g)
    return h.reshape(n_batch, seq, D_MODEL)
```

```python
import functools
import math

import jax
import jax.numpy as jnp
from jax import lax
from jax.experimental import pallas as pl
from jax.experimental.pallas import tpu as pltpu

F32 = jnp.float32
BF16 = jnp.bfloat16

D_MODEL = 1024
GRID_W = 64
HEAD_DIM = 64
ROPE_BASE = 10000.0
EPS = 1e-6
N_MOD = 9
N_BRANCH = 4
BRANCH_W = D_MODEL // 2
A_Q_HEADS = BRANCH_W // HEAD_DIM
A_KV_HEADS = 2
A_GROUP = A_Q_HEADS // A_KV_HEADS
CONV_W = 31
C_HEADS = BRANCH_W // (2 * HEAD_DIM)
D_GROUPS = 4
D_GROUP_CH = BRANCH_W // D_GROUPS
D_FF = 256 * ((8 * D_MODEL // 3 + 255) // 256)
ATTN_SCALE = HEAD_DIM ** -0.5

LANES = 128
SUBLANES = 8
COND_ROWS = 16
TOKEN_TILE = 512
FFN_TILE = 1024
FFN_ROW_GROUPS = 2
FF_CHUNK = 256
CONV_TILE = 256
CONV_HALO = 16
CONV_ROWS = 32
ATTN_Q_ROWS = 1024
ATTN_KEY_CHUNK = 1024
Q_SCALE = ATTN_SCALE * math.log2(math.e)
DFT_ROW_TILE = 512
VMEM_LIMIT = 56 * 1024 * 1024


def _cparams(sem):
    return pltpu.CompilerParams(dimension_semantics=sem, vmem_limit_bytes=VMEM_LIMIT)


def _sigmoid(x):
    return 1.0 / (1.0 + jnp.exp(-x))


def _const_spec(shape, lead=()):
    nd = len(shape)
    return pl.BlockSpec((None,) * len(lead) + tuple(shape), lambda *_: tuple(lead) + (0,) * nd,
                        pipeline_mode=pl.Buffered(1))


def _ada_kernel(cond_ref, w_ref, b_ref, o_ref):
    cnd = cond_ref[...]
    s = cnd * _sigmoid(cnd)
    o_ref[...] = jnp.dot(s, w_ref[...], preferred_element_type=F32,
                         precision=lax.Precision.HIGHEST) + b_ref[...]


def _adaln(cond, ada_w, ada_b):
    depth = ada_w.shape[0]
    n_out = N_MOD * D_MODEL
    col = D_MODEL
    return pl.pallas_call(
        _ada_kernel,
        grid=(depth, n_out // col),
        in_specs=[
            pl.BlockSpec((COND_ROWS, D_MODEL), lambda l, j: (0, 0)),
            pl.BlockSpec((None, D_MODEL, col), lambda l, j: (l, 0, j)),
            pl.BlockSpec((None, 1, col), lambda l, j: (l, 0, j)),
        ],
        out_specs=pl.BlockSpec((None, COND_ROWS, col), lambda l, j: (l, 0, j)),
        out_shape=jax.ShapeDtypeStruct((depth, COND_ROWS, n_out), F32),
        compiler_params=_cparams(("arbitrary", "arbitrary")),
        name="adaln",
    )(cond, ada_w, ada_b.reshape(depth, 1, n_out))


def _rms(x, g):
    return x * lax.rsqrt(jnp.mean(x * x, axis=-1, keepdims=True) + EPS) * g


def _mod_spec(layer, sub, lat_tiles, tiles_per_batch, ctx_row):
    def idx(i):
        row = jnp.where(i < lat_tiles, i // tiles_per_batch, ctx_row)
        return (layer, row, sub, 0, 0)
    return pl.BlockSpec((None, None, None, 3, D_MODEL), idx)


def _ffn_kernel(*refs, final, lat_tiles, split):
    refs = list(refs)
    h_ref = refs.pop(0)
    ctx_ref = refs.pop(0) if split else None
    mod_ref, g_ref, w1_ref, w3_ref, w2_ref = refs[:5]
    fg_ref = refs[5] if final else None
    o_ref = refs[-1]
    shift, scale, gate = mod_ref[0:1, :], mod_ref[1:2, :], mod_ref[2:3, :]
    c = FF_CHUNK
    rows = h_ref.shape[0] // FFN_ROW_GROUPS
    xs, ns = [], []
    for r in range(FFN_ROW_GROUPS):
        x = h_ref[r * rows:(r + 1) * rows, :]
        if split:
            x = jnp.where(pl.program_id(0) < lat_tiles, x, ctx_ref[r * rows:(r + 1) * rows, :])
        xs.append(x)
        ns.append((_rms(x, g_ref[...]) * (1.0 + scale) + shift).astype(BF16))
    for r in range(FFN_ROW_GROUPS):
        acc = jnp.zeros(xs[r].shape, F32)
        for j in range(D_FF // c):
            a = jnp.dot(ns[r], w1_ref[:, c * j:c * (j + 1)], preferred_element_type=F32)
            b = jnp.dot(ns[r], w3_ref[:, c * j:c * (j + 1)], preferred_element_type=F32)
            p = (a * _sigmoid(a) * b).astype(BF16)
            acc = acc + jnp.dot(p, w2_ref[c * j:c * (j + 1), :], preferred_element_type=F32)
        out = xs[r] + 0.5 * gate * acc
        if final:
            out = _rms(out, fg_ref[...])
        o_ref[r * rows:(r + 1) * rows, :] = out


def _half_ffn(h, mods, layer, sub, g, w1, w3, w2, wsel, n_tiles, lat_tiles, tiles_per_batch, ctx_row,
              final_g=None, h_ctx=None):
    tm = FFN_TILE
    group = FFN_TILE // TOKEN_TILE
    n_tiles, lat_tiles, tiles_per_batch = (
        n_tiles // group, lat_tiles // group, tiles_per_batch // group)
    final = final_g is not None
    split = h_ctx is not None
    if split:
        in_specs = [pl.BlockSpec((tm, D_MODEL), lambda i: (jnp.minimum(i, lat_tiles - 1), 0)),
                    pl.BlockSpec((tm, D_MODEL), lambda i: (jnp.maximum(i - lat_tiles, 0), 0))]
        args = [h, h_ctx]
    else:
        in_specs = [pl.BlockSpec((tm, D_MODEL), lambda i: (i, 0))]
        args = [h]
    in_specs += [
        _mod_spec(layer, sub, lat_tiles, tiles_per_batch, ctx_row),
        _const_spec((1, D_MODEL)),
        _const_spec((D_MODEL, D_FF), wsel),
        _const_spec((D_MODEL, D_FF), wsel),
        _const_spec((D_FF, D_MODEL), wsel),
    ]
    args += [mods, g.reshape(1, D_MODEL), w1, w3, w2]
    if final:
        in_specs.append(_const_spec((1, D_MODEL)))
        args.append(final_g.reshape(1, D_MODEL))
    return pl.pallas_call(
        functools.partial(_ffn_kernel, final=final, lat_tiles=lat_tiles, split=split),
        grid=(n_tiles,),
        in_specs=in_specs,
        out_specs=pl.BlockSpec((tm, D_MODEL), lambda i: (i, 0)),
        out_shape=jax.ShapeDtypeStruct((n_tiles * tm, D_MODEL), F32),
        compiler_params=_cparams(("parallel",)),
        name=f"ffn_l{layer}_s{sub}",
    )(*args)


_IN_SEGS = (("aq", 512), ("ak", 128), ("av", 128), ("ba", 512), ("bg", 512),
            ("cq", 512), ("ck", 512), ("cv", 512), ("du", 512), ("gl", 4096))
_IN_OFF = {}
_o = 0
for _n, _s in _IN_SEGS:
    _IN_OFF[_n] = (_o, _o + _s)
    _o += _s
IN_COLS = _o


def _rope_blocks(x, cos, sin_s, low16):
    outs = []
    for j in range(x.shape[1] // LANES):
        xb = x[:, j * LANES:(j + 1) * LANES]
        partner = jnp.where(low16, pltpu.roll(xb, LANES - 16, 1), pltpu.roll(xb, 16, 1))
        outs.append(xb * cos + partner * sin_s)
    return outs


def _in_kernel(h_ref, mod_ref, g_ref, w_ref, wq_ref, cos_ref, sin_ref, seg_ref, qkg_ref,
               qa_ref, ka_ref, va_ref, yg_ref, qc_ref, kc_ref, vc_ref, du_ref, gt_ref):
    x = h_ref[...]
    shift, scale = mod_ref[0:1, :], mod_ref[1:2, :]
    n = (_rms(x, g_ref[...]) * (1.0 + scale) + shift).astype(BF16)

    def proj(name, lo=None, hi=None):
        if name == "aq":
            return jnp.dot(n, wq_ref[...], preferred_element_type=F32)
        a, b = _IN_OFF[name]
        if lo is not None:
            a, b = a + lo, a + hi
        return jnp.dot(n, w_ref[:, a:b], preferred_element_type=F32)

    cos, sin_s = cos_ref[...], sin_ref[...]
    lane = lax.broadcasted_iota(jnp.int32, (1, LANES), 1)
    low16 = (lane & 16) == 0
    seg = seg_ref[...]

    def head_norm(blocks, gains):
        rows = blocks[0].shape[0]
        sq = jnp.concatenate([zb * zb for zb in blocks], axis=0).astype(BF16)
        ss = jnp.dot(sq, seg, preferred_element_type=F32)
        return [zb * lax.rsqrt(ss[j * rows:(j + 1) * rows, :] * (1.0 / HEAD_DIM) + EPS) * g
                for j, (zb, g) in enumerate(zip(blocks, gains))]

    zq, zk = proj("aq"), proj("ak")
    nq = BRANCH_W // LANES
    normed = head_norm([zq[:, j * LANES:(j + 1) * LANES] for j in range(nq)] + [zk],
                       [qkg_ref[0:1, :]] * nq + [qkg_ref[1:2, :]])
    q = jnp.concatenate(normed[:nq], axis=1)
    qa_ref[...] = jnp.concatenate(
        [b * Q_SCALE for b in _rope_blocks(q, cos, sin_s, low16)], axis=1).astype(BF16)
    ka_ref[...] = _rope_blocks(normed[nq], cos, sin_s, low16)[0].astype(BF16)
    va_ref[...] = proj("av").astype(BF16)
    yg_ref[...] = proj("ba") * _sigmoid(proj("bg"))
    qc_ref[...] = jnp.concatenate(
        [b * Q_SCALE for b in _rope_blocks(proj("cq"), cos, sin_s, low16)], axis=1).astype(BF16)
    kc_ref[...] = jnp.concatenate(_rope_blocks(proj("ck"), cos, sin_s, low16), axis=1).astype(BF16)
    vc_ref[...] = proj("cv").astype(BF16)
    du_ref[...] = proj("du").astype(BF16)
    for j in range(N_BRANCH):
        gt_ref[:, j * D_MODEL:(j + 1) * D_MODEL] = _sigmoid(
            proj("gl", j * D_MODEL, (j + 1) * D_MODEL)).astype(BF16)


def _in_proj(h, mods, layer, g, w, wq, cos_t, sin_t, seg, qkg, n_tiles, lat_tiles,
             tiles_per_batch, ctx_row):
    tm = TOKEN_TILE
    t = n_tiles * tm

    def rope_idx(i):
        return (jnp.where(i < lat_tiles, i % tiles_per_batch, tiles_per_batch), 0)

    widths = (512, 128, 128, 512, 512, 512, 512, 512, 4096)
    dtypes = (BF16, BF16, BF16, F32, BF16, BF16, BF16, BF16, BF16)
    return pl.pallas_call(
        _in_kernel,
        grid=(n_tiles,),
        in_specs=[
            pl.BlockSpec((tm, D_MODEL), lambda i: (i, 0)),
            _mod_spec(layer, 1, lat_tiles, tiles_per_batch, ctx_row),
            _const_spec((1, D_MODEL)),
            _const_spec((D_MODEL, IN_COLS), (layer,)),
            _const_spec((D_MODEL, BRANCH_W)),
            pl.BlockSpec((tm, LANES), rope_idx),
            pl.BlockSpec((tm, LANES), rope_idx),
            _const_spec((LANES, LANES)),
            _const_spec((2, LANES)),
        ],
        out_specs=[pl.BlockSpec((tm, wd), lambda i: (i, 0)) for wd in widths],
        out_shape=[jax.ShapeDtypeStruct((t, wd), dt) for wd, dt in zip(widths, dtypes)],
        compiler_params=_cparams(("parallel",)),
        name=f"in_proj_l{layer}",
    )(h, mods, g.reshape(1, D_MODEL), w, wq, cos_t, sin_t, seg, qkg)


def _attn_kernel(*refs, kind, key_counts, n_units, lam_init):
    n_src = len(key_counts)
    q_ref = refs[0]
    k_refs, v_refs = refs[1:1 + n_src], refs[1 + n_src:1 + 2 * n_src]
    rest = refs[1 + 2 * n_src:]
    if kind == "gqa":
        o_ref, s_a, s_b, mx_a, mx_b = rest
    else:
        lam_ref, g_ref, o_ref, s_a, s_b, mx_a, mx_b, o0_scr = rest
    t = pl.program_id(0)

    @pl.when(t == 0)
    def _():
        s_b[...] = jnp.zeros(s_b.shape, F32)
        mx_b[...] = jnp.zeros(mx_b.shape, F32)
        if kind == "gqa":
            o_ref[...] = jnp.zeros(o_ref.shape, BF16)
        else:
            o0_scr[...] = jnp.zeros(o0_scr.shape, F32)

    def step(front_half, s_front, mx_front, s_back, mx_back):
        lane_half = lax.broadcasted_iota(jnp.int32, (1, LANES), 1) // HEAD_DIM
        qp = jnp.where(lane_half == front_half, q_ref[...].astype(F32), 0.0).astype(BF16)
        m_back = mx_back[:, 0:1]
        run_max, acc, off = None, None, 0
        for k_ref, v_ref, n_keys in zip(k_refs, v_refs, key_counts):
            for c0 in range(0, n_keys, ATTN_KEY_CHUNK):
                kc = min(ATTN_KEY_CHUNK, n_keys - c0)
                s = lax.dot_general(qp, k_ref[c0:c0 + kc, :], (((1,), (1,)), ((), ())),
                                    preferred_element_type=F32)
                s_front[:, off:off + kc] = s
                part = functools.reduce(
                    jnp.maximum, [s[:, j * LANES:(j + 1) * LANES] for j in range(kc // LANES)])
                run_max = part if run_max is None else jnp.maximum(run_max, part)
                p = jnp.exp2(s_back[:, off:off + kc] - m_back).astype(BF16)
                v_ext = jnp.concatenate(
                    [v_ref[c0:c0 + kc, :], jnp.ones((kc, LANES), BF16)], axis=1)
                pv = jnp.dot(p, v_ext, preferred_element_type=F32)
                acc = pv if acc is None else acc + pv
                off += kc
        mx_front[...] = jnp.broadcast_to(jnp.max(run_max, axis=-1, keepdims=True), run_max.shape)
        o = acc[:, :LANES] / acc[:, LANES:]
        back_half = 1 - front_half
        if kind == "gqa":
            if back_half == 0:
                o_ref[...] = o.astype(BF16)
            else:
                low = lax.broadcasted_iota(jnp.int32, (1, LANES), 1) < HEAD_DIM
                o_ref[...] = jnp.where(low, o_ref[...], o.astype(BF16))
        elif back_half == 0:
            o0_scr[...] = o
        else:
            lp = lam_ref[...]
            lam = (jnp.exp(jnp.sum(lp[0:1, :] * lp[1:2, :], axis=-1, keepdims=True))
                   - jnp.exp(jnp.sum(lp[2:3, :] * lp[3:4, :], axis=-1, keepdims=True)) + lam_init)
            o_ref[...] = (_rms(o0_scr[...] - lam * o, g_ref[...]) * (1.0 - lam_init)).astype(BF16)

    @pl.when(t % 2 == 0)
    def _():
        step(0, s_a, mx_a, s_b, mx_b)

    @pl.when(t % 2 == 1)
    def _():
        step(1, s_b, mx_b, s_a, mx_a)


def _attention(kind, q, k, v, extra, n_batch, q_rows, q_row0, sources, lam_init=None, name=""):
    tq = min(ATTN_Q_ROWS, q_rows)
    n_q = q_rows // tq
    per_tile = 2 * BRANCH_W // LANES
    n_units = n_batch * n_q * per_tile
    kv_col = (lambda blk: 0) if kind == "gqa" else (lambda blk: blk)

    def decode(x):
        return x // (n_q * per_tile), (x // per_tile) % n_q, (x % per_tile) // 2

    front = lambda t: decode(jnp.minimum(t, n_units - 1))
    back = lambda t: decode(jnp.maximum(t - 1, 0))

    def q_idx(t):
        b, i, blk = front(t)
        return (q_row0 // tq + b * n_q + i, blk)

    def o_idx(t):
        b, i, blk = back(t)
        return (b * n_q + i, blk)

    def kv_idx(t, which, r):
        b, _, blk = which(t)
        return (r + b, kv_col(blk))

    in_specs = [pl.BlockSpec((tq, LANES), q_idx)]
    args = [q]
    for arr, which in ((k, front), (v, back)):
        for row0, n_keys in sources:
            in_specs.append(pl.BlockSpec(
                (n_keys, LANES), functools.partial(kv_idx, which=which, r=row0 // n_keys)))
            args.append(arr)
    total_keys = sum(nk for _, nk in sources)
    scratch = [pltpu.VMEM((tq, total_keys), F32), pltpu.VMEM((tq, total_keys), F32),
               pltpu.VMEM((tq, LANES), F32), pltpu.VMEM((tq, LANES), F32)]
    if kind == "diff":
        in_specs += [pl.BlockSpec(extra[0].shape, lambda t: (0, 0)),
                     pl.BlockSpec(extra[1].shape, lambda t: (0, 0))]
        args += list(extra)
        scratch.append(pltpu.VMEM((tq, LANES), F32))
    return pl.pallas_call(
        functools.partial(_attn_kernel, kind=kind, key_counts=tuple(nk for _, nk in sources),
                          n_units=n_units, lam_init=lam_init),
        grid=(n_units + 1,),
        in_specs=in_specs,
        out_specs=pl.BlockSpec((tq, LANES), o_idx),
        out_shape=jax.ShapeDtypeStruct((n_batch * q_rows, BRANCH_W), BF16),
        scratch_shapes=scratch,
        compiler_params=_cparams(("arbitrary",)),
        name=name,
    )(*args)


def _conv_kernel(prev_ref, cur_ref, next_ref, w_ref, b_ref, g_ref, beta_ref, o_ref, win_ref,
                 stage_ref,
                 *, lat_tiles, lat_per_seq, ctx_per_seq):
    i = pl.program_id(0)
    j = jnp.where(i < lat_tiles, i % lat_per_seq, (i - lat_tiles) % ctx_per_seq)
    per = jnp.where(i < lat_tiles, lat_per_seq, ctx_per_seq)
    first, last = j == 0, j == per - 1
    hl, tm = CONV_HALO, CONV_TILE
    win_ref[0:hl, :] = jnp.where(first, 0.0, prev_ref[...])
    win_ref[hl:hl + tm, :] = cur_ref[...]
    win_ref[hl + tm:hl + tm + hl, :] = jnp.where(last, 0.0, next_ref[...])
    half = CONV_W // 2
    rc = CONV_ROWS
    span = stage_ref.shape[1]
    for phase in range(SUBLANES):
        stage_ref[phase] = win_ref[phase:phase + span, :]
    for c in range(tm // rc):
        acc = jnp.zeros((rc // SUBLANES, SUBLANES, BRANCH_W), F32)
        for k in range(CONV_W):
            off = k + hl - half
            r0 = c * rc + off - off % SUBLANES
            xs = stage_ref[off % SUBLANES, r0:r0 + rc, :]
            acc = acc + xs.reshape(rc // SUBLANES, SUBLANES, BRANCH_W) * w_ref[k]
        y = acc.reshape(rc, BRANCH_W) + b_ref[...]
        mu = jnp.mean(y, axis=-1, keepdims=True)
        yc = y - mu
        var = jnp.mean(yc * yc, axis=-1, keepdims=True)
        yn = yc * lax.rsqrt(var + EPS) * g_ref[...] + beta_ref[...]
        o_ref[c * rc:(c + 1) * rc, :] = (yn * _sigmoid(yn)).astype(BF16)


def _conv_module(y, w, b, g, beta, n_rows, lat_rows, seq, ctx_len):
    tm, hl = CONV_TILE, CONV_HALO
    n_tiles = n_rows // tm
    r = tm // hl
    n_halo = y.shape[0] // hl
    body = functools.partial(_conv_kernel, lat_tiles=lat_rows // tm, lat_per_seq=seq // tm,
                             ctx_per_seq=max(ctx_len // tm, 1))
    vec = lambda a: a.reshape(1, BRANCH_W)
    return pl.pallas_call(
        body,
        grid=(n_tiles,),
        in_specs=[
            pl.BlockSpec((hl, BRANCH_W), lambda i: (jnp.maximum(i * r - 1, 0), 0)),
            pl.BlockSpec((tm, BRANCH_W), lambda i: (i, 0)),
            pl.BlockSpec((hl, BRANCH_W), lambda i: (jnp.minimum((i + 1) * r, n_halo - 1), 0)),
            _const_spec((CONV_W, SUBLANES, BRANCH_W)),
            _const_spec((1, BRANCH_W)), _const_spec((1, BRANCH_W)), _const_spec((1, BRANCH_W)),
        ],
        out_specs=pl.BlockSpec((tm, BRANCH_W), lambda i: (i, 0)),
        out_shape=jax.ShapeDtypeStruct((n_rows, BRANCH_W), BF16),
        scratch_shapes=[pltpu.VMEM((tm + 2 * hl, BRANCH_W), F32),
                        pltpu.VMEM((SUBLANES, tm + 2 * hl - SUBLANES, BRANCH_W), F32)],
        compiler_params=_cparams(("parallel",)),
        name="conv_module",
    )(y, y, y, jnp.broadcast_to(w[:, None, :], (CONV_W, SUBLANES, BRANCH_W)),
      vec(b), vec(g), vec(beta))


def _dft_kernel(u_ref, cc_ref, sc_ref, flip_ref, t_ref, o_ref, z_ref, mid_ref, *, seq, tile):
    half = seq // 2
    n_fold = half // tile

    @pl.when(pl.program_id(1) == 0)
    def _():
        row = lax.broadcasted_iota(jnp.int32, (tile, 1), 0)
        for j in range(n_fold):
            src = (2 * n_fold - 1 - j) * tile
            rev = jnp.dot(flip_ref[...], u_ref[src:src + tile, :], preferred_element_type=F32)
            if j > 0:
                rev = jnp.where(row == 0, u_ref[src + tile:src + tile + 1, :].astype(F32), rev)
            x = u_ref[j * tile:(j + 1) * tile, :].astype(F32)
            plus, minus = (x + rev).astype(BF16), (x - rev).astype(BF16)
            for gidx in range(D_GROUPS):
                cols = slice(gidx * LANES, (gidx + 1) * LANES)
                z_ref[j * tile:(j + 1) * tile, cols] = jnp.dot(
                    plus[:, cols], cc_ref[...], preferred_element_type=F32).astype(BF16)
                z_ref[half + j * tile:half + (j + 1) * tile, cols] = jnp.dot(
                    minus[:, cols], sc_ref[...], preferred_element_type=F32).astype(BF16)
        for gidx in range(D_GROUPS):
            cols = slice(gidx * LANES, (gidx + 1) * LANES)
            mid_ref[:, cols] = jnp.dot(u_ref[half:half + mid_ref.shape[0], cols], cc_ref[...],
                                       preferred_element_type=F32)

    k_odd = lax.broadcasted_iota(jnp.int32, (tile, 1), 0) % 2
    sign = 1.0 - 2.0 * k_odd.astype(F32)
    o_ref[...] = (jnp.dot(t_ref[...], z_ref[...], preferred_element_type=F32)
                  + sign * mid_ref[0:1, :]).astype(BF16)


def _dft_tables(seq):
    def cs_rows(rows, n, period):
        ang = ((rows[:, None] * jnp.arange(n, dtype=jnp.int32)[None, :]) % period).astype(F32)
        ang = ang * (2.0 * math.pi / period)
        return jnp.cos(ang), jnp.sin(ang)

    def cs(n):
        split = 64
        if n <= split or n % split:
            return cs_rows(jnp.arange(n, dtype=jnp.int32), n, n)
        j = jnp.arange(split, dtype=jnp.int32)
        ac, as_ = cs_rows(jnp.arange(n // split, dtype=jnp.int32), n, n // split)
        bc, bs = cs_rows(j, n, n)
        cos = ac[:, None, :] * bc[None, :, :] - as_[:, None, :] * bs[None, :, :]
        sin = as_[:, None, :] * bc[None, :, :] + ac[:, None, :] * bs[None, :, :]
        return cos.reshape(n, n), sin.reshape(n, n)
    ct, st = cs(seq)
    cc, sc = cs(D_GROUP_CH)
    norm = 1.0 / math.sqrt(seq * D_GROUP_CH)
    half = seq // 2
    tmat = jnp.concatenate([ct[:, :half], -st[:, :half]], axis=1).astype(BF16)
    tile = min(DFT_ROW_TILE, half)
    i = jnp.arange(tile, dtype=jnp.int32)
    flip = ((i[:, None] + i[None, :] == tile) & (i[:, None] > 0)).astype(BF16)
    return (cc * norm).astype(BF16), (sc * norm).astype(BF16), flip, tmat


def _fourier_mix(u, tables, n_batch, seq, row0, name):
    cc, sc, flip, tmat = tables
    tr = flip.shape[0]
    n_r = seq // tr
    mid_rows = 2 * SUBLANES
    return pl.pallas_call(
        functools.partial(_dft_kernel, seq=seq, tile=tr),
        grid=(n_batch, n_r),
        in_specs=[
            pl.BlockSpec((seq, BRANCH_W), lambda b, r: (row0 // seq + b, 0)),
            pl.BlockSpec((LANES, LANES), lambda b, r: (0, 0)),
            pl.BlockSpec((LANES, LANES), lambda b, r: (0, 0)),
            pl.BlockSpec((tr, tr), lambda b, r: (0, 0)),
            pl.BlockSpec((tr, seq), lambda b, r: (r, 0)),
        ],
        out_specs=pl.BlockSpec((tr, BRANCH_W), lambda b, r: (b * n_r + r, 0)),
        out_shape=jax.ShapeDtypeStruct((n_batch * seq, BRANCH_W), BF16),
        scratch_shapes=[pltpu.VMEM((seq, BRANCH_W), BF16), pltpu.VMEM((mid_rows, BRANCH_W), F32)],
        compiler_params=_cparams(("parallel", "arbitrary")),
        name=name,
    )(u, cc, sc, flip, tmat)


def _merge_kernel(*refs, has_ctx, lat_tiles):
    refs = list(refs)
    n_y = sum(2 if c else 1 for c in has_ctx)
    y_refs, (gt_ref, h_ref, mod_ref, wba_ref, wb_ref, wo_ref, o_ref) = refs[:n_y], refs[n_y:]
    is_lat = pl.program_id(0) < lat_tiles
    acc = None
    for j, split in enumerate(has_ctx):
        y = y_refs.pop(0)[...]
        if split:
            y = jnp.where(is_lat, y, y_refs.pop(0)[...])
        w = wba_ref[...] if j == 0 else wb_ref[j]
        term = gt_ref[:, j * D_MODEL:(j + 1) * D_MODEL].astype(F32) * jnp.dot(
            y, w, preferred_element_type=F32)
        acc = term if acc is None else acc + term
    y = jnp.dot(acc.astype(BF16), wo_ref[...], preferred_element_type=F32)
    o_ref[...] = h_ref[...] + mod_ref[2:3, :] * y


def _merge(ys, gates, h, mods, layer, wb_a, wb, wo, n_tiles, lat_tiles, tiles_per_batch, ctx_row):
    tm = TOKEN_TILE
    row = lambda wd: pl.BlockSpec((tm, wd), lambda i: (i, 0))
    y_specs, y_args, has_ctx = [], [], []
    for y in ys:
        split = isinstance(y, tuple)
        has_ctx.append(split)
        if split:
            y_specs += [pl.BlockSpec((tm, BRANCH_W), lambda i: (jnp.minimum(i, lat_tiles - 1), 0)),
                        pl.BlockSpec((tm, BRANCH_W), lambda i: (jnp.maximum(i - lat_tiles, 0), 0))]
            y_args += list(y)
        else:
            y_specs.append(row(BRANCH_W))
            y_args.append(y)
    return pl.pallas_call(
        functools.partial(_merge_kernel, has_ctx=tuple(has_ctx), lat_tiles=lat_tiles),
        grid=(n_tiles,),
        in_specs=y_specs + [
            row(N_BRANCH * D_MODEL), row(D_MODEL),
            _mod_spec(layer, 1, lat_tiles, tiles_per_batch, ctx_row),
            _const_spec((BRANCH_W, D_MODEL)),
            _const_spec((N_BRANCH, BRANCH_W, D_MODEL), (layer,)),
            _const_spec((D_MODEL, D_MODEL), (layer,)),
        ],
        out_specs=row(D_MODEL),
        out_shape=jax.ShapeDtypeStruct((n_tiles * tm, D_MODEL), F32),
        compiler_params=_cparams(("parallel",)),
        name=f"merge_l{layer}",
    )(*y_args, gates, h, mods, wb_a, wb, wo)


def _rope_tables(seq, pad_rows):
    pos = jnp.arange(seq, dtype=jnp.int32)
    row = (pos // GRID_W).astype(F32)
    col = (pos % GRID_W).astype(F32)
    half = HEAD_DIM // 2
    inv = ROPE_BASE ** (-jnp.arange(0, half, 2, dtype=F32) / half)
    ang_r, ang_c = row[:, None] * inv, col[:, None] * inv
    ang = jnp.concatenate([ang_r, ang_r, ang_c, ang_c], axis=-1)
    cos, sin = jnp.cos(ang), jnp.sin(ang)
    lane = jnp.arange(HEAD_DIM)
    sin_s = jnp.where((lane & 16) == 0, -sin, sin)
    cos = jnp.concatenate([cos, jnp.ones((pad_rows, HEAD_DIM), F32)], axis=0)
    sin_s = jnp.concatenate([sin_s, jnp.zeros((pad_rows, HEAD_DIM), F32)], axis=0)
    return jnp.tile(cos, (1, 2)), jnp.tile(sin_s, (1, 2))


def _paired_query_weight(w_in):
    aq = w_in[:, :BRANCH_W].reshape(D_MODEL, A_KV_HEADS, A_GROUP, HEAD_DIM)
    return jnp.transpose(aq, (0, 2, 1, 3)).reshape(D_MODEL, BRANCH_W).astype(BF16)


def kernel(x, c, ctx, c_ctx, ada_w, ada_b, norm_g, ffn_w1, ffn_w3, ffn_w2, w_in, qk_norm_a,
           conv_w, conv_b, conv_ln_g, conv_ln_b, diff_lam, diff_subln_g, w_branch, w_out, final_g):
    n_batch, seq, _ = x.shape
    ctx_len = ctx.shape[1]
    depth = ada_w.shape[0]
    tm = TOKEN_TILE
    lat_rows, ctx_rows = n_batch * seq, n_batch * ctx_len
    assert seq % FFN_TILE == 0 and ctx_rows % FFN_TILE == 0 and ctx_len % CONV_TILE == 0
    assert n_batch < COND_ROWS and seq % GRID_W == 0
    lat_tiles, all_tiles = lat_rows // tm, (lat_rows + ctx_rows) // tm
    tiles_per_batch = seq // tm
    ctx_row = n_batch

    cond = jnp.concatenate(
        [c, c_ctx[None, :], jnp.zeros((COND_ROWS - n_batch - 1, D_MODEL), F32)], axis=0)
    mods = _adaln(cond, ada_w, ada_b).reshape(depth, COND_ROWS, 3, 3, D_MODEL)

    cos_t, sin_t = _rope_tables(seq, tm)
    lane = jnp.arange(LANES)
    seg = (lane[:, None] // HEAD_DIM == lane[None, :] // HEAD_DIM).astype(BF16)
    dft_lat = _dft_tables(seq)
    dft_ctx = _dft_tables(ctx_len)
    tile_args = (lat_tiles, tiles_per_batch, ctx_row)

    h, h_ctx = x.reshape(lat_rows, D_MODEL), ctx.reshape(ctx_rows, D_MODEL)
    w1_bf, w3_bf, w2_bf = ffn_w1.astype(BF16), ffn_w3.astype(BF16), ffn_w2.astype(BF16)
    w_in_bf, wb_bf, wo_bf = w_in.astype(BF16), w_branch.astype(BF16), w_out.astype(BF16)
    for l in range(depth):
        update_ctx = l < depth - 1
        lam_init = 0.8 - 0.6 * math.exp(-0.3 * l)
        out_tiles = all_tiles if update_ctx else lat_tiles
        out_rows = out_tiles * tm
        wb_a = wb_bf[l, 0].reshape(A_KV_HEADS, A_GROUP, HEAD_DIM, D_MODEL)
        wb_a = jnp.transpose(wb_a, (1, 0, 2, 3)).reshape(BRANCH_W, D_MODEL)
        qkg = jnp.tile(qk_norm_a[l], (1, 2))

        h = _half_ffn(h, mods, l, 0, norm_g[l, 0], w1_bf, w3_bf, w2_bf, (l, 0),
                      all_tiles, *tile_args, h_ctx=h_ctx)
        h_ctx = None
        qa, ka, va, yg, qc, kc, vc, du, gates = _in_proj(
            h, mods, l, norm_g[l, 1], w_in_bf, _paired_query_weight(w_in[l]), cos_t, sin_t,
            seg, qkg, all_tiles, *tile_args)

        lat_src = ((0, seq), (lat_rows, ctx_len))
        ctx_src = ((lat_rows, ctx_len),)
        diff_extra = (diff_lam[l], diff_subln_g[l].reshape(1, LANES))
        y_a = _attention("gqa", qa, ka, va, None, n_batch, seq, 0, lat_src, name=f"gqa_l{l}")
        y_c = _attention("diff", qc, kc, vc, diff_extra, n_batch, seq, 0, lat_src,
                         lam_init=lam_init, name=f"diff_l{l}")
        y_b = _conv_module(yg, conv_w[l], conv_b[l], conv_ln_g[l], conv_ln_b[l],
                           out_rows, lat_rows, seq, ctx_len)
        y_d = _fourier_mix(du, dft_lat, n_batch, seq, 0, f"dft_l{l}")
        if update_ctx:
            y_a = (y_a, _attention("gqa", qa, ka, va, None, n_batch, ctx_len, lat_rows, ctx_src,
                                   name=f"gqa_ctx_l{l}"))
            y_c = (y_c, _attention("diff", qc, kc, vc, diff_extra, n_batch, ctx_len, lat_rows,
                                   ctx_src, lam_init=lam_init, name=f"diff_ctx_l{l}"))
            y_d = (y_d, _fourier_mix(du, dft_ctx, n_batch, ctx_len, lat_rows, f"dft_ctx_l{l}"))
        h = _merge((y_a, y_b, y_c, y_d), gates, h, mods, l, wb_a, wb_bf, wo_bf,
                   out_tiles, *tile_args)
        h = _half_ffn(h, mods, l, 2, norm_g[l, 2], w1_bf, w3_bf, w2_bf, (l, 1),
                      out_tiles, *tile_args, final_g=None if update_ctx else final_g)
    return h.reshape(n_batch, seq, D_MODEL)
```

```python
import functools
import math

import jax
import jax.numpy as jnp
from jax import lax
from jax.experimental import pallas as pl
from jax.experimental.pallas import tpu as pltpu

F32 = jnp.float32
BF16 = jnp.bfloat16

D_MODEL = 1024
GRID_W = 64
HEAD_DIM = 64
ROPE_BASE = 10000.0
EPS = 1e-6
N_MOD = 9
N_BRANCH = 4
BRANCH_W = D_MODEL // 2
A_Q_HEADS = BRANCH_W // HEAD_DIM
A_KV_HEADS = 2
A_GROUP = A_Q_HEADS // A_KV_HEADS
CONV_W = 31
C_HEADS = BRANCH_W // (2 * HEAD_DIM)
D_GROUPS = 4
D_GROUP_CH = BRANCH_W // D_GROUPS
D_FF = 256 * ((8 * D_MODEL // 3 + 255) // 256)
ATTN_SCALE = HEAD_DIM ** -0.5

LANES = 128
SUBLANES = 8
COND_ROWS = 16
TOKEN_TILE = 512
FFN_TILE = 1024
FFN_ROW_GROUPS = 2
FF_CHUNK = 256
CONV_TILE = 256
CONV_HALO = 16
CONV_ROWS = 32
ATTN_Q_ROWS = 1024
ATTN_KEY_CHUNK = 1024
Q_SCALE = ATTN_SCALE * math.log2(math.e)
DFT_ROW_TILE = 512
VMEM_LIMIT = 56 * 1024 * 1024


def _cparams(sem):
    return pltpu.CompilerParams(dimension_semantics=sem, vmem_limit_bytes=VMEM_LIMIT)


def _sigmoid(x):
    return 1.0 / (1.0 + jnp.exp(-x))


def _const_spec(shape, lead=()):
    nd = len(shape)
    return pl.BlockSpec((None,) * len(lead) + tuple(shape), lambda *_: tuple(lead) + (0,) * nd,
                        pipeline_mode=pl.Buffered(1))


def _ada_kernel(cond_ref, w_ref, b_ref, o_ref):
    cnd = cond_ref[...]
    s = cnd * _sigmoid(cnd)
    o_ref[...] = jnp.dot(s, w_ref[...], preferred_element_type=F32,
                         precision=lax.Precision.HIGHEST) + b_ref[...]


def _adaln(cond, ada_w, ada_b):
    depth = ada_w.shape[0]
    n_out = N_MOD * D_MODEL
    col = D_MODEL
    return pl.pallas_call(
        _ada_kernel,
        grid=(depth, n_out // col),
        in_specs=[
            pl.BlockSpec((COND_ROWS, D_MODEL), lambda l, j: (0, 0)),
            pl.BlockSpec((None, D_MODEL, col), lambda l, j: (l, 0, j)),
            pl.BlockSpec((None, 1, col), lambda l, j: (l, 0, j)),
        ],
        out_specs=pl.BlockSpec((None, COND_ROWS, col), lambda l, j: (l, 0, j)),
        out_shape=jax.ShapeDtypeStruct((depth, COND_ROWS, n_out), F32),
        compiler_params=_cparams(("arbitrary", "arbitrary")),
        name="adaln",
    )(cond, ada_w, ada_b.reshape(depth, 1, n_out))


def _rms(x, g):
    return x * lax.rsqrt(jnp.mean(x * x, axis=-1, keepdims=True) + EPS) * g


def _mod_spec(layer, sub, lat_tiles, tiles_per_batch, ctx_row):
    def idx(i):
        row = jnp.where(i < lat_tiles, i // tiles_per_batch, ctx_row)
        return (layer, row, sub, 0, 0)
    return pl.BlockSpec((None, None, None, 3, D_MODEL), idx)


def _ffn_kernel(*refs, final, lat_tiles, split):
    refs = list(refs)
    h_ref = refs.pop(0)
    ctx_ref = refs.pop(0) if split else None
    mod_ref, g_ref, w1_ref, w3_ref, w2_ref = refs[:5]
    fg_ref = refs[5] if final else None
    o_ref = refs[-1]
    shift, scale, gate = mod_ref[0:1, :], mod_ref[1:2, :], mod_ref[2:3, :]
    c = FF_CHUNK
    rows = h_ref.shape[0] // FFN_ROW_GROUPS
    xs, ns = [], []
    for r in range(FFN_ROW_GROUPS):
        x = h_ref[r * rows:(r + 1) * rows, :]
        if split:
            x = jnp.where(pl.program_id(0) < lat_tiles, x, ctx_ref[r * rows:(r + 1) * rows, :])
        xs.append(x)
        ns.append((_rms(x, g_ref[...]) * (1.0 + scale) + shift).astype(BF16))
    for r in range(FFN_ROW_GROUPS):
        acc = jnp.zeros(xs[r].shape, F32)
        for j in range(D_FF // c):
            a = jnp.dot(ns[r], w1_ref[:, c * j:c * (j + 1)], preferred_element_type=F32)
            b = jnp.dot(ns[r], w3_ref[:, c * j:c * (j + 1)], preferred_element_type=F32)
            p = (a * _sigmoid(a) * b).astype(BF16)
            acc = acc + jnp.dot(p, w2_ref[c * j:c * (j + 1), :], preferred_element_type=F32)
        out = xs[r] + 0.5 * gate * acc
        if final:
            out = _rms(out, fg_ref[...])
        o_ref[r * rows:(r + 1) * rows, :] = out


def _half_ffn(h, mods, layer, sub, g, w1, w3, w2, wsel, n_tiles, lat_tiles, tiles_per_batch, ctx_row,
              final_g=None, h_ctx=None):
    tm = FFN_TILE
    group = FFN_TILE // TOKEN_TILE
    n_tiles, lat_tiles, tiles_per_batch = (
        n_tiles // group, lat_tiles // group, tiles_per_batch // group)
    final = final_g is not None
    split = h_ctx is not None
    if split:
        in_specs = [pl.BlockSpec((tm, D_MODEL), lambda i: (jnp.minimum(i, lat_tiles - 1), 0)),
                    pl.BlockSpec((tm, D_MODEL), lambda i: (jnp.maximum(i - lat_tiles, 0), 0))]
        args = [h, h_ctx]
    else:
        in_specs = [pl.BlockSpec((tm, D_MODEL), lambda i: (i, 0))]
        args = [h]
    in_specs += [
        _mod_spec(layer, sub, lat_tiles, tiles_per_batch, ctx_row),
        _const_spec((1, D_MODEL)),
        _const_spec((D_MODEL, D_FF), wsel),
        _const_spec((D_MODEL, D_FF), wsel),
        _const_spec((D_FF, D_MODEL), wsel),
    ]
    args += [mods, g.reshape(1, D_MODEL), w1, w3, w2]
    if final:
        in_specs.append(_const_spec((1, D_MODEL)))
        args.append(final_g.reshape(1, D_MODEL))
    return pl.pallas_call(
        functools.partial(_ffn_kernel, final=final, lat_tiles=lat_tiles, split=split),
        grid=(n_tiles,),
        in_specs=in_specs,
        out_specs=pl.BlockSpec((tm, D_MODEL), lambda i: (i, 0)),
        out_shape=jax.ShapeDtypeStruct((n_tiles * tm, D_MODEL), F32),
        compiler_params=_cparams(("parallel",)),
        name=f"ffn_l{layer}_s{sub}",
    )(*args)


_IN_SEGS = (("aq", 512), ("ak", 128), ("av", 128), ("ba", 512), ("bg", 512),
            ("cq", 512), ("ck", 512), ("cv", 512), ("du", 512), ("gl", 4096))
_IN_OFF = {}
_o = 0
for _n, _s in _IN_SEGS:
    _IN_OFF[_n] = (_o, _o + _s)
    _o += _s
IN_COLS = _o


def _rope_blocks(x, cos, sin_s, low16):
    outs = []
    for j in range(x.shape[1] // LANES):
        xb = x[:, j * LANES:(j + 1) * LANES]
        partner = jnp.where(low16, pltpu.roll(xb, LANES - 16, 1), pltpu.roll(xb, 16, 1))
        outs.append(xb * cos + partner * sin_s)
    return outs


def _in_kernel(h_ref, mod_ref, g_ref, w_ref, wq_ref, cos_ref, sin_ref, seg_ref, qkg_ref,
               qa_ref, ka_ref, va_ref, yg_ref, qc_ref, kc_ref, vc_ref, du_ref, gt_ref):
    x = h_ref[...]
    shift, scale = mod_ref[0:1, :], mod_ref[1:2, :]
    n = (_rms(x, g_ref[...]) * (1.0 + scale) + shift).astype(BF16)

    def proj(name, lo=None, hi=None):
        if name == "aq":
            return jnp.dot(n, wq_ref[...], preferred_element_type=F32)
        a, b = _IN_OFF[name]
        if lo is not None:
            a, b = a + lo, a + hi
        return jnp.dot(n, w_ref[:, a:b], preferred_element_type=F32)

    cos, sin_s = cos_ref[...], sin_ref[...]
    lane = lax.broadcasted_iota(jnp.int32, (1, LANES), 1)
    low16 = (lane & 16) == 0
    seg = seg_ref[...]

    def head_norm(blocks, gains):
        rows = blocks[0].shape[0]
        sq = jnp.concatenate([zb * zb for zb in blocks], axis=0).astype(BF16)
        ss = jnp.dot(sq, seg, preferred_element_type=F32)
        return [zb * lax.rsqrt(ss[j * rows:(j + 1) * rows, :] * (1.0 / HEAD_DIM) + EPS) * g
                for j, (zb, g) in enumerate(zip(blocks, gains))]

    zq, zk = proj("aq"), proj("ak")
    nq = BRANCH_W // LANES
    normed = head_norm([zq[:, j * LANES:(j + 1) * LANES] for j in range(nq)] + [zk],
                       [qkg_ref[0:1, :]] * nq + [qkg_ref[1:2, :]])
    q = jnp.concatenate(normed[:nq], axis=1)
    qa_ref[...] = jnp.concatenate(
        [b * Q_SCALE for b in _rope_blocks(q, cos, sin_s, low16)], axis=1).astype(BF16)
    ka_ref[...] = _rope_blocks(normed[nq], cos, sin_s, low16)[0].astype(BF16)
    va_ref[...] = proj("av").astype(BF16)
    yg_ref[...] = proj("ba") * _sigmoid(proj("bg"))
    qc_ref[...] = jnp.concatenate(
        [b * Q_SCALE for b in _rope_blocks(proj("cq"), cos, sin_s, low16)], axis=1).astype(BF16)
    kc_ref[...] = jnp.concatenate(_rope_blocks(proj("ck"), cos, sin_s, low16), axis=1).astype(BF16)
    vc_ref[...] = proj("cv").astype(BF16)
    du_ref[...] = proj("du").astype(BF16)
    for j in range(N_BRANCH):
        gt_ref[:, j * D_MODEL:(j + 1) * D_MODEL] = _sigmoid(
            proj("gl", j * D_MODEL, (j + 1) * D_MODEL)).astype(BF16)


def _in_proj(h, mods, layer, g, w, wq, cos_t, sin_t, seg, qkg, n_tiles, lat_tiles,
             tiles_per_batch, ctx_row):
    tm = TOKEN_TILE
    t = n_tiles * tm

    def rope_idx(i):
        return (jnp.where(i < lat_tiles, i % tiles_per_batch, tiles_per_batch), 0)

    widths = (512, 128, 128, 512, 512, 512, 512, 512, 4096)
    dtypes = (BF16, BF16, BF16, F32, BF16, BF16, BF16, BF16, BF16)
    return pl.pallas_call(
        _in_kernel,
        grid=(n_tiles,),
        in_specs=[
            pl.BlockSpec((tm, D_MODEL), lambda i: (i, 0)),
            _mod_spec(layer, 1, lat_tiles, tiles_per_batch, ctx_row),
            _const_spec((1, D_MODEL)),
            _const_spec((D_MODEL, IN_COLS), (layer,)),
            _const_spec((D_MODEL, BRANCH_W)),
            pl.BlockSpec((tm, LANES), rope_idx),
            pl.BlockSpec((tm, LANES), rope_idx),
            _const_spec((LANES, LANES)),
            _const_spec((2, LANES)),
        ],
        out_specs=[pl.BlockSpec((tm, wd), lambda i: (i, 0)) for wd in widths],
        out_shape=[jax.ShapeDtypeStruct((t, wd), dt) for wd, dt in zip(widths, dtypes)],
        compiler_params=_cparams(("parallel",)),
        name=f"in_proj_l{layer}",
    )(h, mods, g.reshape(1, D_MODEL), w, wq, cos_t, sin_t, seg, qkg)


def _attn_kernel(*refs, kind, key_counts, n_units, lam_init, conv_tiles_per_seq):
    n_src = len(key_counts)
    q_ref = refs[0]
    k_refs, v_refs = refs[1:1 + n_src], refs[1 + n_src:1 + 2 * n_src]
    rest = list(refs[1 + 2 * n_src:])
    if kind == "diff":
        lam_ref, g_ref = rest.pop(0), rest.pop(0)
    conv_in = [rest.pop(0) for _ in range(7)] if conv_tiles_per_seq else None
    o_ref = rest.pop(0)
    conv_out = rest.pop(0) if conv_tiles_per_seq else None
    s_a, s_b, mx_a, mx_b = rest[:4]
    if kind == "diff":
        o0_scr = rest[4]
    conv_scratch = rest[-2:] if conv_tiles_per_seq else None
    t = pl.program_id(0)

    @pl.when(t == 0)
    def _():
        s_b[...] = jnp.zeros(s_b.shape, F32)
        mx_b[...] = jnp.zeros(mx_b.shape, F32)
        if kind == "gqa":
            o_ref[...] = jnp.zeros(o_ref.shape, BF16)
        else:
            o0_scr[...] = jnp.zeros(o0_scr.shape, F32)

    def step(front_half, s_front, mx_front, s_back, mx_back):
        lane_half = lax.broadcasted_iota(jnp.int32, (1, LANES), 1) // HEAD_DIM
        qp = jnp.where(lane_half == front_half, q_ref[...].astype(F32), 0.0).astype(BF16)
        m_back = mx_back[:, 0:1]
        run_max, acc, off = None, None, 0
        side_work = []
        if conv_tiles_per_seq:
            conv_tile = jnp.minimum(t, n_units - 1)
            side_work = _conv_items(conv_tile % conv_tiles_per_seq, conv_tiles_per_seq,
                                    *conv_in, conv_out, *conv_scratch)
        for k_ref, v_ref, n_keys in zip(k_refs, v_refs, key_counts):
            for c0 in range(0, n_keys, ATTN_KEY_CHUNK):
                if side_work:
                    side_work.pop(0)()
                kc = min(ATTN_KEY_CHUNK, n_keys - c0)
                s = lax.dot_general(qp, k_ref[c0:c0 + kc, :], (((1,), (1,)), ((), ())),
                                    preferred_element_type=F32)
                s_front[:, off:off + kc] = s
                part = functools.reduce(
                    jnp.maximum, [s[:, j * LANES:(j + 1) * LANES] for j in range(kc // LANES)])
                run_max = part if run_max is None else jnp.maximum(run_max, part)
                p = jnp.exp2(s_back[:, off:off + kc] - m_back).astype(BF16)
                v_ext = jnp.concatenate(
                    [v_ref[c0:c0 + kc, :], jnp.ones((kc, LANES), BF16)], axis=1)
                pv = jnp.dot(p, v_ext, preferred_element_type=F32)
                acc = pv if acc is None else acc + pv
                off += kc
        for item in side_work:
            item()
        mx_front[...] = jnp.broadcast_to(jnp.max(run_max, axis=-1, keepdims=True), run_max.shape)
        o = acc[:, :LANES] / acc[:, LANES:]
        back_half = 1 - front_half
        if kind == "gqa":
            if back_half == 0:
                o_ref[...] = o.astype(BF16)
            else:
                low = lax.broadcasted_iota(jnp.int32, (1, LANES), 1) < HEAD_DIM
                o_ref[...] = jnp.where(low, o_ref[...], o.astype(BF16))
        elif back_half == 0:
            o0_scr[...] = o
        else:
            lp = lam_ref[...]
            lam = (jnp.exp(jnp.sum(lp[0:1, :] * lp[1:2, :], axis=-1, keepdims=True))
                   - jnp.exp(jnp.sum(lp[2:3, :] * lp[3:4, :], axis=-1, keepdims=True)) + lam_init)
            o_ref[...] = (_rms(o0_scr[...] - lam * o, g_ref[...]) * (1.0 - lam_init)).astype(BF16)

    @pl.when(t % 2 == 0)
    def _():
        step(0, s_a, mx_a, s_b, mx_b)

    @pl.when(t % 2 == 1)
    def _():
        step(1, s_b, mx_b, s_a, mx_a)


def _attention(kind, q, k, v, extra, n_batch, q_rows, q_row0, sources, lam_init=None, name="",
               conv=None):
    tq = min(ATTN_Q_ROWS, q_rows)
    n_q = q_rows // tq
    per_tile = 2 * BRANCH_W // LANES
    n_units = n_batch * n_q * per_tile
    kv_col = (lambda blk: 0) if kind == "gqa" else (lambda blk: blk)

    def decode(x):
        return x // (n_q * per_tile), (x // per_tile) % n_q, (x % per_tile) // 2

    front = lambda t: decode(jnp.minimum(t, n_units - 1))
    back = lambda t: decode(jnp.maximum(t - 1, 0))

    def q_idx(t):
        b, i, blk = front(t)
        return (q_row0 // tq + b * n_q + i, blk)

    def o_idx(t):
        b, i, blk = back(t)
        return (b * n_q + i, blk)

    def kv_idx(t, which, r):
        b, _, blk = which(t)
        return (r + b, kv_col(blk))

    in_specs = [pl.BlockSpec((tq, LANES), q_idx)]
    args = [q]
    for arr, which in ((k, front), (v, back)):
        for row0, n_keys in sources:
            in_specs.append(pl.BlockSpec(
                (n_keys, LANES), functools.partial(kv_idx, which=which, r=row0 // n_keys)))
            args.append(arr)
    total_keys = sum(nk for _, nk in sources)
    scratch = [pltpu.VMEM((tq, total_keys), F32), pltpu.VMEM((tq, total_keys), F32),
               pltpu.VMEM((tq, LANES), F32), pltpu.VMEM((tq, LANES), F32)]
    if kind == "diff":
        in_specs += [pl.BlockSpec(extra[0].shape, lambda t: (0, 0)),
                     pl.BlockSpec(extra[1].shape, lambda t: (0, 0))]
        args += list(extra)
        scratch.append(pltpu.VMEM((tq, LANES), F32))
    out_specs = [pl.BlockSpec((tq, LANES), o_idx)]
    out_shape = [jax.ShapeDtypeStruct((n_batch * q_rows, BRANCH_W), BF16)]
    conv_tiles_per_seq = None
    if conv is not None:
        y, conv_params = conv
        conv_tile = n_batch * q_rows // n_units
        assert q_row0 == 0 and conv_tile % CONV_ROWS == 0 and conv_tile >= CONV_HALO
        conv_tiles_per_seq = q_rows // conv_tile
        tile_idx = lambda t: jnp.minimum(t, n_units - 1)
        specs, y_args, conv_scratch = _conv_specs(y, conv_tile, tile_idx)
        in_specs += specs
        args += y_args + _conv_args(*conv_params)
        scratch += conv_scratch
        out_specs.append(pl.BlockSpec((conv_tile, BRANCH_W), lambda t: (tile_idx(t), 0)))
        out_shape.append(jax.ShapeDtypeStruct((n_batch * q_rows, BRANCH_W), BF16))
    outs = pl.pallas_call(
        functools.partial(_attn_kernel, kind=kind, key_counts=tuple(nk for _, nk in sources),
                          n_units=n_units, lam_init=lam_init,
                          conv_tiles_per_seq=conv_tiles_per_seq),
        grid=(n_units + 1,),
        in_specs=in_specs,
        out_specs=out_specs,
        out_shape=out_shape,
        scratch_shapes=scratch,
        compiler_params=_cparams(("arbitrary",)),
        name=name,
    )(*args)
    return outs[0] if conv is None else outs


def _conv_items(tile_in_seq, tiles_per_seq, prev_ref, cur_ref, next_ref, w_ref, b_ref, g_ref,
                beta_ref, o_ref, win_ref, stage_ref):
    hl, tm = CONV_HALO, cur_ref.shape[0]
    half, rc = CONV_W // 2, CONV_ROWS

    def stage():
        first, last = tile_in_seq == 0, tile_in_seq == tiles_per_seq - 1
        win_ref[0:hl, :] = jnp.where(first, 0.0, prev_ref[...])
        win_ref[hl:hl + tm, :] = cur_ref[...]
        win_ref[hl + tm:hl + tm + hl, :] = jnp.where(last, 0.0, next_ref[...])
        span = stage_ref.shape[1]
        for phase in range(SUBLANES):
            stage_ref[phase] = win_ref[phase:phase + span, :]

    def chunk(c):
        acc = jnp.zeros((rc // SUBLANES, SUBLANES, BRANCH_W), F32)
        for k in range(CONV_W):
            off = k + hl - half
            r0 = c * rc + off - off % SUBLANES
            xs = stage_ref[off % SUBLANES, r0:r0 + rc, :]
            acc = acc + xs.reshape(rc // SUBLANES, SUBLANES, BRANCH_W) * w_ref[k]
        y = acc.reshape(rc, BRANCH_W) + b_ref[...]
        mu = jnp.mean(y, axis=-1, keepdims=True)
        yc = y - mu
        var = jnp.mean(yc * yc, axis=-1, keepdims=True)
        yn = yc * lax.rsqrt(var + EPS) * g_ref[...] + beta_ref[...]
        o_ref[c * rc:(c + 1) * rc, :] = (yn * _sigmoid(yn)).astype(BF16)

    return [stage] + [functools.partial(chunk, c) for c in range(tm // rc)]


def _conv_specs(y, tile, tile_idx):
    hl = CONV_HALO
    r, n_halo = tile // hl, y.shape[0] // hl
    specs = [
        pl.BlockSpec((hl, BRANCH_W), lambda *a: (jnp.maximum(tile_idx(*a) * r - 1, 0), 0)),
        pl.BlockSpec((tile, BRANCH_W), lambda *a: (tile_idx(*a), 0)),
        pl.BlockSpec((hl, BRANCH_W),
                     lambda *a: (jnp.minimum((tile_idx(*a) + 1) * r, n_halo - 1), 0)),
        _const_spec((CONV_W, SUBLANES, BRANCH_W)),
        _const_spec((1, BRANCH_W)), _const_spec((1, BRANCH_W)), _const_spec((1, BRANCH_W)),
    ]
    scratch = [pltpu.VMEM((tile + 2 * hl, BRANCH_W), F32),
               pltpu.VMEM((SUBLANES, tile + 2 * hl - SUBLANES, BRANCH_W), F32)]
    return specs, [y, y, y], scratch


def _conv_args(w, b, g, beta):
    vec = lambda a: a.reshape(1, BRANCH_W)
    return [jnp.broadcast_to(w[:, None, :], (CONV_W, SUBLANES, BRANCH_W)), vec(b), vec(g),
            vec(beta)]


def _conv_kernel(*refs, tile0, tiles_per_seq):
    tile = tile0 + pl.program_id(0)
    for item in _conv_items(tile % tiles_per_seq, tiles_per_seq, *refs):
        item()


def _conv_module(y, w, b, g, beta, row0, n_rows, seq_len):
    tm = CONV_TILE
    tile0 = row0 // tm
    specs, y_args, scratch = _conv_specs(y, tm, lambda i: tile0 + i)
    return pl.pallas_call(
        functools.partial(_conv_kernel, tile0=tile0, tiles_per_seq=seq_len // tm),
        grid=(n_rows // tm,),
        in_specs=specs,
        out_specs=pl.BlockSpec((tm, BRANCH_W), lambda i: (i, 0)),
        out_shape=jax.ShapeDtypeStruct((n_rows, BRANCH_W), BF16),
        scratch_shapes=scratch,
        compiler_params=_cparams(("parallel",)),
        name="conv_module",
    )(*y_args, *_conv_args(w, b, g, beta))


def _dft_kernel(u_ref, cc_ref, sc_ref, flip_ref, t_ref, o_ref, z_ref, mid_ref, *, seq, tile):
    half = seq // 2
    n_fold = half // tile

    @pl.when(pl.program_id(1) == 0)
    def _():
        row = lax.broadcasted_iota(jnp.int32, (tile, 1), 0)
        for j in range(n_fold):
            src = (2 * n_fold - 1 - j) * tile
            rev = jnp.dot(flip_ref[...], u_ref[src:src + tile, :], preferred_element_type=F32)
            if j > 0:
                rev = jnp.where(row == 0, u_ref[src + tile:src + tile + 1, :].astype(F32), rev)
            x = u_ref[j * tile:(j + 1) * tile, :].astype(F32)
            plus, minus = (x + rev).astype(BF16), (x - rev).astype(BF16)
            for gidx in range(D_GROUPS):
                cols = slice(gidx * LANES, (gidx + 1) * LANES)
                z_ref[j * tile:(j + 1) * tile, cols] = jnp.dot(
                    plus[:, cols], cc_ref[...], preferred_element_type=F32).astype(BF16)
                z_ref[half + j * tile:half + (j + 1) * tile, cols] = jnp.dot(
                    minus[:, cols], sc_ref[...], preferred_element_type=F32).astype(BF16)
        for gidx in range(D_GROUPS):
            cols = slice(gidx * LANES, (gidx + 1) * LANES)
            mid_ref[:, cols] = jnp.dot(u_ref[half:half + mid_ref.shape[0], cols], cc_ref[...],
                                       preferred_element_type=F32)

    k_odd = lax.broadcasted_iota(jnp.int32, (tile, 1), 0) % 2
    sign = 1.0 - 2.0 * k_odd.astype(F32)
    o_ref[...] = (jnp.dot(t_ref[...], z_ref[...], preferred_element_type=F32)
                  + sign * mid_ref[0:1, :]).astype(BF16)


def _dft_tables(seq):
    def cs_rows(rows, n, period):
        ang = ((rows[:, None] * jnp.arange(n, dtype=jnp.int32)[None, :]) % period).astype(F32)
        ang = ang * (2.0 * math.pi / period)
        return jnp.cos(ang), jnp.sin(ang)

    def cs(n):
        split = 64
        if n <= split or n % split:
            return cs_rows(jnp.arange(n, dtype=jnp.int32), n, n)
        j = jnp.arange(split, dtype=jnp.int32)
        ac, as_ = cs_rows(jnp.arange(n // split, dtype=jnp.int32), n, n // split)
        bc, bs = cs_rows(j, n, n)
        cos = ac[:, None, :] * bc[None, :, :] - as_[:, None, :] * bs[None, :, :]
        sin = as_[:, None, :] * bc[None, :, :] + ac[:, None, :] * bs[None, :, :]
        return cos.reshape(n, n), sin.reshape(n, n)
    ct, st = cs(seq)
    cc, sc = cs(D_GROUP_CH)
    norm = 1.0 / math.sqrt(seq * D_GROUP_CH)
    half = seq // 2
    tmat = jnp.concatenate([ct[:, :half], -st[:, :half]], axis=1).astype(BF16)
    tile = min(DFT_ROW_TILE, half)
    i = jnp.arange(tile, dtype=jnp.int32)
    flip = ((i[:, None] + i[None, :] == tile) & (i[:, None] > 0)).astype(BF16)
    return (cc * norm).astype(BF16), (sc * norm).astype(BF16), flip, tmat


def _fourier_mix(u, tables, n_batch, seq, row0, name):
    cc, sc, flip, tmat = tables
    tr = flip.shape[0]
    n_r = seq // tr
    mid_rows = 2 * SUBLANES
    return pl.pallas_call(
        functools.partial(_dft_kernel, seq=seq, tile=tr),
        grid=(n_batch, n_r),
        in_specs=[
            pl.BlockSpec((seq, BRANCH_W), lambda b, r: (row0 // seq + b, 0)),
            pl.BlockSpec((LANES, LANES), lambda b, r: (0, 0)),
            pl.BlockSpec((LANES, LANES), lambda b, r: (0, 0)),
            pl.BlockSpec((tr, tr), lambda b, r: (0, 0)),
            pl.BlockSpec((tr, seq), lambda b, r: (r, 0)),
        ],
        out_specs=pl.BlockSpec((tr, BRANCH_W), lambda b, r: (b * n_r + r, 0)),
        out_shape=jax.ShapeDtypeStruct((n_batch * seq, BRANCH_W), BF16),
        scratch_shapes=[pltpu.VMEM((seq, BRANCH_W), BF16), pltpu.VMEM((mid_rows, BRANCH_W), F32)],
        compiler_params=_cparams(("parallel", "arbitrary")),
        name=name,
    )(u, cc, sc, flip, tmat)


def _merge_kernel(*refs, has_ctx, lat_tiles):
    refs = list(refs)
    n_y = sum(2 if c else 1 for c in has_ctx)
    y_refs, (gt_ref, h_ref, mod_ref, wba_ref, wb_ref, wo_ref, o_ref) = refs[:n_y], refs[n_y:]
    is_lat = pl.program_id(0) < lat_tiles
    acc = None
    for j, split in enumerate(has_ctx):
        y = y_refs.pop(0)[...]
        if split:
            y = jnp.where(is_lat, y, y_refs.pop(0)[...])
        w = wba_ref[...] if j == 0 else wb_ref[j]
        term = gt_ref[:, j * D_MODEL:(j + 1) * D_MODEL].astype(F32) * jnp.dot(
            y, w, preferred_element_type=F32)
        acc = term if acc is None else acc + term
    y = jnp.dot(acc.astype(BF16), wo_ref[...], preferred_element_type=F32)
    o_ref[...] = h_ref[...] + mod_ref[2:3, :] * y


def _merge(ys, gates, h, mods, layer, wb_a, wb, wo, n_tiles, lat_tiles, tiles_per_batch, ctx_row):
    tm = TOKEN_TILE
    row = lambda wd: pl.BlockSpec((tm, wd), lambda i: (i, 0))
    y_specs, y_args, has_ctx = [], [], []
    for y in ys:
        split = isinstance(y, tuple)
        has_ctx.append(split)
        if split:
            y_specs += [pl.BlockSpec((tm, BRANCH_W), lambda i: (jnp.minimum(i, lat_tiles - 1), 0)),
                        pl.BlockSpec((tm, BRANCH_W), lambda i: (jnp.maximum(i - lat_tiles, 0), 0))]
            y_args += list(y)
        else:
            y_specs.append(row(BRANCH_W))
            y_args.append(y)
    return pl.pallas_call(
        functools.partial(_merge_kernel, has_ctx=tuple(has_ctx), lat_tiles=lat_tiles),
        grid=(n_tiles,),
        in_specs=y_specs + [
            row(N_BRANCH * D_MODEL), row(D_MODEL),
            _mod_spec(layer, 1, lat_tiles, tiles_per_batch, ctx_row),
            _const_spec((BRANCH_W, D_MODEL)),
            _const_spec((N_BRANCH, BRANCH_W, D_MODEL), (layer,)),
            _const_spec((D_MODEL, D_MODEL), (layer,)),
        ],
        out_specs=row(D_MODEL),
        out_shape=jax.ShapeDtypeStruct((n_tiles * tm, D_MODEL), F32),
        compiler_params=_cparams(("parallel",)),
        name=f"merge_l{layer}",
    )(*y_args, gates, h, mods, wb_a, wb, wo)


def _rope_tables(seq, pad_rows):
    pos = jnp.arange(seq, dtype=jnp.int32)
    row = (pos // GRID_W).astype(F32)
    col = (pos % GRID_W).astype(F32)
    half = HEAD_DIM // 2
    inv = ROPE_BASE ** (-jnp.arange(0, half, 2, dtype=F32) / half)
    ang_r, ang_c = row[:, None] * inv, col[:, None] * inv
    ang = jnp.concatenate([ang_r, ang_r, ang_c, ang_c], axis=-1)
    cos, sin = jnp.cos(ang), jnp.sin(ang)
    lane = jnp.arange(HEAD_DIM)
    sin_s = jnp.where((lane & 16) == 0, -sin, sin)
    cos = jnp.concatenate([cos, jnp.ones((pad_rows, HEAD_DIM), F32)], axis=0)
    sin_s = jnp.concatenate([sin_s, jnp.zeros((pad_rows, HEAD_DIM), F32)], axis=0)
    return jnp.tile(cos, (1, 2)), jnp.tile(sin_s, (1, 2))


def _paired_query_weight(w_in):
    aq = w_in[:, :BRANCH_W].reshape(D_MODEL, A_KV_HEADS, A_GROUP, HEAD_DIM)
    return jnp.transpose(aq, (0, 2, 1, 3)).reshape(D_MODEL, BRANCH_W).astype(BF16)


def kernel(x, c, ctx, c_ctx, ada_w, ada_b, norm_g, ffn_w1, ffn_w3, ffn_w2, w_in, qk_norm_a,
           conv_w, conv_b, conv_ln_g, conv_ln_b, diff_lam, diff_subln_g, w_branch, w_out, final_g):
    n_batch, seq, _ = x.shape
    ctx_len = ctx.shape[1]
    depth = ada_w.shape[0]
    tm = TOKEN_TILE
    lat_rows, ctx_rows = n_batch * seq, n_batch * ctx_len
    assert seq % FFN_TILE == 0 and ctx_rows % FFN_TILE == 0 and ctx_len % CONV_TILE == 0
    assert n_batch < COND_ROWS and seq % GRID_W == 0
    lat_tiles, all_tiles = lat_rows // tm, (lat_rows + ctx_rows) // tm
    tiles_per_batch = seq // tm
    ctx_row = n_batch

    cond = jnp.concatenate(
        [c, c_ctx[None, :], jnp.zeros((COND_ROWS - n_batch - 1, D_MODEL), F32)], axis=0)
    mods = _adaln(cond, ada_w, ada_b).reshape(depth, COND_ROWS, 3, 3, D_MODEL)

    cos_t, sin_t = _rope_tables(seq, tm)
    lane = jnp.arange(LANES)
    seg = (lane[:, None] // HEAD_DIM == lane[None, :] // HEAD_DIM).astype(BF16)
    dft_lat = _dft_tables(seq)
    dft_ctx = _dft_tables(ctx_len)
    tile_args = (lat_tiles, tiles_per_batch, ctx_row)

    h, h_ctx = x.reshape(lat_rows, D_MODEL), ctx.reshape(ctx_rows, D_MODEL)
    w1_bf, w3_bf, w2_bf = ffn_w1.astype(BF16), ffn_w3.astype(BF16), ffn_w2.astype(BF16)
    w_in_bf, wb_bf, wo_bf = w_in.astype(BF16), w_branch.astype(BF16), w_out.astype(BF16)
    for l in range(depth):
        update_ctx = l < depth - 1
        lam_init = 0.8 - 0.6 * math.exp(-0.3 * l)
        out_tiles = all_tiles if update_ctx else lat_tiles
        wb_a = wb_bf[l, 0].reshape(A_KV_HEADS, A_GROUP, HEAD_DIM, D_MODEL)
        wb_a = jnp.transpose(wb_a, (1, 0, 2, 3)).reshape(BRANCH_W, D_MODEL)
        qkg = jnp.tile(qk_norm_a[l], (1, 2))

        h = _half_ffn(h, mods, l, 0, norm_g[l, 0], w1_bf, w3_bf, w2_bf, (l, 0),
                      all_tiles, *tile_args, h_ctx=h_ctx)
        h_ctx = None
        qa, ka, va, yg, qc, kc, vc, du, gates = _in_proj(
            h, mods, l, norm_g[l, 1], w_in_bf, _paired_query_weight(w_in[l]), cos_t, sin_t,
            seg, qkg, all_tiles, *tile_args)

        lat_src = ((0, seq), (lat_rows, ctx_len))
        ctx_src = ((lat_rows, ctx_len),)
        diff_extra = (diff_lam[l], diff_subln_g[l].reshape(1, LANES))
        conv_params = (conv_w[l], conv_b[l], conv_ln_g[l], conv_ln_b[l])
        y_a, y_b = _attention("gqa", qa, ka, va, None, n_batch, seq, 0, lat_src,
                              name=f"gqa_l{l}", conv=(yg, conv_params))
        y_c = _attention("diff", qc, kc, vc, diff_extra, n_batch, seq, 0, lat_src,
                         lam_init=lam_init, name=f"diff_l{l}")
        y_d = _fourier_mix(du, dft_lat, n_batch, seq, 0, f"dft_l{l}")
        if update_ctx:
            y_b = (y_b, _conv_module(yg, *conv_params, lat_rows, ctx_rows, ctx_len))
            y_a = (y_a, _attention("gqa", qa, ka, va, None, n_batch, ctx_len, lat_rows, ctx_src,
                                   name=f"gqa_ctx_l{l}"))
            y_c = (y_c, _attention("diff", qc, kc, vc, diff_extra, n_batch, ctx_len, lat_rows,
                                   ctx_src, lam_init=lam_init, name=f"diff_ctx_l{l}"))
            y_d = (y_d, _fourier_mix(du, dft_ctx, n_batch, ctx_len, lat_rows, f"dft_ctx_l{l}"))
        h = _merge((y_a, y_b, y_c, y_d), gates, h, mods, l, wb_a, wb_bf, wo_bf,
                   out_tiles, *tile_args)
        h = _half_ffn(h, mods, l, 2, norm_g[l, 2], w1_bf, w3_bf, w2_bf, (l, 1),
                      out_tiles, *tile_args, final_g=None if update_ctx else final_g)
    return h.reshape(n_batch, seq, D_MODEL)
```

```python
import functools
import math

import jax
import jax.numpy as jnp
from jax import lax
from jax.experimental import pallas as pl
from jax.experimental.pallas import tpu as pltpu

F32 = jnp.float32
BF16 = jnp.bfloat16

D_MODEL = 1024
GRID_W = 64
HEAD_DIM = 64
ROPE_BASE = 10000.0
EPS = 1e-6
N_MOD = 9
N_BRANCH = 4
BRANCH_W = D_MODEL // 2
A_Q_HEADS = BRANCH_W // HEAD_DIM
A_KV_HEADS = 2
A_GROUP = A_Q_HEADS // A_KV_HEADS
CONV_W = 31
C_HEADS = BRANCH_W // (2 * HEAD_DIM)
D_GROUPS = 4
D_GROUP_CH = BRANCH_W // D_GROUPS
D_FF = 256 * ((8 * D_MODEL // 3 + 255) // 256)
ATTN_SCALE = HEAD_DIM ** -0.5

LANES = 128
SUBLANES = 8
COND_ROWS = 16
TOKEN_TILE = 512
FF_CHUNK = 256
CONV_TILE = 256
CONV_HALO = 16
CONV_ROWS = 32
ATTN_Q_ROWS = 1024
ATTN_KEY_CHUNK = 1024
Q_SCALE = ATTN_SCALE * math.log2(math.e)
DFT_ROW_TILE = 512
VMEM_LIMIT = 56 * 1024 * 1024


def _cparams(sem):
    return pltpu.CompilerParams(dimension_semantics=sem, vmem_limit_bytes=VMEM_LIMIT)


def _sigmoid(x):
    return 1.0 / (1.0 + jnp.exp(-x))


def _const_spec(shape, lead=()):
    nd = len(shape)
    return pl.BlockSpec((None,) * len(lead) + tuple(shape), lambda *_: tuple(lead) + (0,) * nd,
                        pipeline_mode=pl.Buffered(1))


def _ada_kernel(cond_ref, w_ref, b_ref, o_ref):
    cnd = cond_ref[...]
    s = cnd * _sigmoid(cnd)
    o_ref[...] = jnp.dot(s, w_ref[...], preferred_element_type=F32,
                         precision=lax.Precision.HIGHEST) + b_ref[...]


def _adaln(cond, ada_w, ada_b):
    depth = ada_w.shape[0]
    n_out = N_MOD * D_MODEL
    col = D_MODEL
    return pl.pallas_call(
        _ada_kernel,
        grid=(depth, n_out // col),
        in_specs=[
            pl.BlockSpec((COND_ROWS, D_MODEL), lambda l, j: (0, 0)),
            pl.BlockSpec((None, D_MODEL, col), lambda l, j: (l, 0, j)),
            pl.BlockSpec((None, 1, col), lambda l, j: (l, 0, j)),
        ],
        out_specs=pl.BlockSpec((None, COND_ROWS, col), lambda l, j: (l, 0, j)),
        out_shape=jax.ShapeDtypeStruct((depth, COND_ROWS, n_out), F32),
        compiler_params=_cparams(("arbitrary", "arbitrary")),
        name="adaln",
    )(cond, ada_w, ada_b.reshape(depth, 1, n_out))


def _rms(x, g):
    return x * lax.rsqrt(jnp.mean(x * x, axis=-1, keepdims=True) + EPS) * g


def _mod_spec(layer, sub, lat_tiles, tiles_per_batch, ctx_row):
    def idx(i):
        row = jnp.where(i < lat_tiles, i // tiles_per_batch, ctx_row)
        return (layer, row, sub, 0, 0)
    return pl.BlockSpec((None, None, None, 3, D_MODEL), idx)


def _ffn_kernel(*refs, final, lat_tiles, split):
    refs = list(refs)
    h_ref = refs.pop(0)
    ctx_ref = refs.pop(0) if split else None
    mod_ref, g_ref, w1_ref, w3_ref, w2_ref = refs[:5]
    fg_ref = refs[5] if final else None
    o_ref = refs[-1]
    x = h_ref[...]
    if split:
        x = jnp.where(pl.program_id(0) < lat_tiles, x, ctx_ref[...])
    shift, scale, gate = mod_ref[0:1, :], mod_ref[1:2, :], mod_ref[2:3, :]
    n = (_rms(x, g_ref[...]) * (1.0 + scale) + shift).astype(BF16)
    acc = jnp.zeros(x.shape, F32)
    c = FF_CHUNK
    for j in range(D_FF // c):
        a = jnp.dot(n, w1_ref[:, c * j:c * (j + 1)], preferred_element_type=F32)
        b = jnp.dot(n, w3_ref[:, c * j:c * (j + 1)], preferred_element_type=F32)
        p = (a * _sigmoid(a) * b).astype(BF16)
        acc = acc + jnp.dot(p, w2_ref[c * j:c * (j + 1), :], preferred_element_type=F32)
    out = x + 0.5 * gate * acc
    if final:
        out = _rms(out, fg_ref[...])
    o_ref[...] = out


def _half_ffn(h, mods, layer, sub, g, w1, w3, w2, wsel, n_tiles, lat_tiles, tiles_per_batch, ctx_row,
              final_g=None, h_ctx=None):
    tm = TOKEN_TILE
    final = final_g is not None
    split = h_ctx is not None
    if split:
        in_specs = [pl.BlockSpec((tm, D_MODEL), lambda i: (jnp.minimum(i, lat_tiles - 1), 0)),
                    pl.BlockSpec((tm, D_MODEL), lambda i: (jnp.maximum(i - lat_tiles, 0), 0))]
        args = [h, h_ctx]
    else:
        in_specs = [pl.BlockSpec((tm, D_MODEL), lambda i: (i, 0))]
        args = [h]
    in_specs += [
        _mod_spec(layer, sub, lat_tiles, tiles_per_batch, ctx_row),
        _const_spec((1, D_MODEL)),
        _const_spec((D_MODEL, D_FF), wsel),
        _const_spec((D_MODEL, D_FF), wsel),
        _const_spec((D_FF, D_MODEL), wsel),
    ]
    args += [mods, g.reshape(1, D_MODEL), w1, w3, w2]
    if final:
        in_specs.append(_const_spec((1, D_MODEL)))
        args.append(final_g.reshape(1, D_MODEL))
    return pl.pallas_call(
        functools.partial(_ffn_kernel, final=final, lat_tiles=lat_tiles, split=split),
        grid=(n_tiles,),
        in_specs=in_specs,
        out_specs=pl.BlockSpec((tm, D_MODEL), lambda i: (i, 0)),
        out_shape=jax.ShapeDtypeStruct((n_tiles * tm, D_MODEL), F32),
        compiler_params=_cparams(("parallel",)),
        name=f"ffn_l{layer}_s{sub}",
    )(*args)


_IN_SEGS = (("aq", 512), ("ak", 128), ("av", 128), ("ba", 512), ("bg", 512),
            ("cq", 512), ("ck", 512), ("cv", 512), ("du", 512), ("gl", 4096))
_IN_OFF = {}
_o = 0
for _n, _s in _IN_SEGS:
    _IN_OFF[_n] = (_o, _o + _s)
    _o += _s
IN_COLS = _o


def _rope_blocks(x, cos, sin_s, low16):
    outs = []
    for j in range(x.shape[1] // LANES):
        xb = x[:, j * LANES:(j + 1) * LANES]
        partner = jnp.where(low16, pltpu.roll(xb, LANES - 16, 1), pltpu.roll(xb, 16, 1))
        outs.append(xb * cos + partner * sin_s)
    return outs


def _in_kernel(h_ref, mod_ref, g_ref, w_ref, wq_ref, cos_ref, sin_ref, seg_ref, qkg_ref,
               qa_ref, ka_ref, va_ref, yg_ref, qc_ref, kc_ref, vc_ref, du_ref, gt_ref):
    x = h_ref[...]
    shift, scale = mod_ref[0:1, :], mod_ref[1:2, :]
    n = (_rms(x, g_ref[...]) * (1.0 + scale) + shift).astype(BF16)

    def proj(name, lo=None, hi=None):
        if name == "aq":
            return jnp.dot(n, wq_ref[...], preferred_element_type=F32)
        a, b = _IN_OFF[name]
        if lo is not None:
            a, b = a + lo, a + hi
        return jnp.dot(n, w_ref[:, a:b], preferred_element_type=F32)

    cos, sin_s = cos_ref[...], sin_ref[...]
    lane = lax.broadcasted_iota(jnp.int32, (1, LANES), 1)
    low16 = (lane & 16) == 0
    seg = seg_ref[...]

    def head_norm(blocks, gains):
        rows = blocks[0].shape[0]
        sq = jnp.concatenate([zb * zb for zb in blocks], axis=0).astype(BF16)
        ss = jnp.dot(sq, seg, preferred_element_type=F32)
        return [zb * lax.rsqrt(ss[j * rows:(j + 1) * rows, :] * (1.0 / HEAD_DIM) + EPS) * g
                for j, (zb, g) in enumerate(zip(blocks, gains))]

    zq, zk = proj("aq"), proj("ak")
    nq = BRANCH_W // LANES
    normed = head_norm([zq[:, j * LANES:(j + 1) * LANES] for j in range(nq)] + [zk],
                       [qkg_ref[0:1, :]] * nq + [qkg_ref[1:2, :]])
    q = jnp.concatenate(normed[:nq], axis=1)
    qa_ref[...] = jnp.concatenate(
        [b * Q_SCALE for b in _rope_blocks(q, cos, sin_s, low16)], axis=1).astype(BF16)
    ka_ref[...] = _rope_blocks(normed[nq], cos, sin_s, low16)[0].astype(BF16)
    va_ref[...] = proj("av").astype(BF16)
    yg_ref[...] = proj("ba") * _sigmoid(proj("bg"))
    qc_ref[...] = jnp.concatenate(
        [b * Q_SCALE for b in _rope_blocks(proj("cq"), cos, sin_s, low16)], axis=1).astype(BF16)
    kc_ref[...] = jnp.concatenate(_rope_blocks(proj("ck"), cos, sin_s, low16), axis=1).astype(BF16)
    vc_ref[...] = proj("cv").astype(BF16)
    du_ref[...] = proj("du").astype(BF16)
    for j in range(N_BRANCH):
        gt_ref[:, j * D_MODEL:(j + 1) * D_MODEL] = _sigmoid(
            proj("gl", j * D_MODEL, (j + 1) * D_MODEL)).astype(BF16)


def _in_proj(h, mods, layer, g, w, wq, cos_t, sin_t, seg, qkg, n_tiles, lat_tiles,
             tiles_per_batch, ctx_row):
    tm = TOKEN_TILE
    t = n_tiles * tm

    def rope_idx(i):
        return (jnp.where(i < lat_tiles, i % tiles_per_batch, tiles_per_batch), 0)

    widths = (512, 128, 128, 512, 512, 512, 512, 512, 4096)
    dtypes = (BF16, BF16, BF16, F32, BF16, BF16, BF16, BF16, BF16)
    return pl.pallas_call(
        _in_kernel,
        grid=(n_tiles,),
        in_specs=[
            pl.BlockSpec((tm, D_MODEL), lambda i: (i, 0)),
            _mod_spec(layer, 1, lat_tiles, tiles_per_batch, ctx_row),
            _const_spec((1, D_MODEL)),
            _const_spec((D_MODEL, IN_COLS), (layer,)),
            _const_spec((D_MODEL, BRANCH_W)),
            pl.BlockSpec((tm, LANES), rope_idx),
            pl.BlockSpec((tm, LANES), rope_idx),
            _const_spec((LANES, LANES)),
            _const_spec((2, LANES)),
        ],
        out_specs=[pl.BlockSpec((tm, wd), lambda i: (i, 0)) for wd in widths],
        out_shape=[jax.ShapeDtypeStruct((t, wd), dt) for wd, dt in zip(widths, dtypes)],
        compiler_params=_cparams(("parallel",)),
        name=f"in_proj_l{layer}",
    )(h, mods, g.reshape(1, D_MODEL), w, wq, cos_t, sin_t, seg, qkg)


def _attn_kernel(*refs, kind, key_counts, n_units, lam_init, conv_tiles_per_seq):
    n_src = len(key_counts)
    q_ref = refs[0]
    k_refs, v_refs = refs[1:1 + n_src], refs[1 + n_src:1 + 2 * n_src]
    rest = list(refs[1 + 2 * n_src:])
    if kind == "diff":
        lam_ref, g_ref = rest.pop(0), rest.pop(0)
    conv_in = [rest.pop(0) for _ in range(7)] if conv_tiles_per_seq else None
    o_ref = rest.pop(0)
    conv_out = rest.pop(0) if conv_tiles_per_seq else None
    s_a, s_b, mx_a, mx_b = rest[:4]
    if kind == "diff":
        o0_scr = rest[4]
    conv_scratch = rest[-2:] if conv_tiles_per_seq else None
    t = pl.program_id(0)

    @pl.when(t == 0)
    def _():
        s_b[...] = jnp.zeros(s_b.shape, F32)
        mx_b[...] = jnp.zeros(mx_b.shape, F32)
        if kind == "gqa":
            o_ref[...] = jnp.zeros(o_ref.shape, BF16)
        else:
            o0_scr[...] = jnp.zeros(o0_scr.shape, F32)

    def step(front_half, s_front, mx_front, s_back, mx_back):
        lane_half = lax.broadcasted_iota(jnp.int32, (1, LANES), 1) // HEAD_DIM
        qp = jnp.where(lane_half == front_half, q_ref[...].astype(F32), 0.0).astype(BF16)
        m_back = mx_back[:, 0:1]
        run_max, acc, off = None, None, 0
        side_work = []
        if conv_tiles_per_seq:
            conv_tile = jnp.minimum(t, n_units - 1)
            side_work = _conv_items(conv_tile % conv_tiles_per_seq, conv_tiles_per_seq,
                                    *conv_in, conv_out, *conv_scratch)
        for k_ref, v_ref, n_keys in zip(k_refs, v_refs, key_counts):
            for c0 in range(0, n_keys, ATTN_KEY_CHUNK):
                if side_work:
                    side_work.pop(0)()
                kc = min(ATTN_KEY_CHUNK, n_keys - c0)
                s = lax.dot_general(qp, k_ref[c0:c0 + kc, :], (((1,), (1,)), ((), ())),
                                    preferred_element_type=F32)
                s_front[:, off:off + kc] = s
                part = functools.reduce(
                    jnp.maximum, [s[:, j * LANES:(j + 1) * LANES] for j in range(kc // LANES)])
                run_max = part if run_max is None else jnp.maximum(run_max, part)
                p = jnp.exp2(s_back[:, off:off + kc] - m_back).astype(BF16)
                v_ext = jnp.concatenate(
                    [v_ref[c0:c0 + kc, :], jnp.ones((kc, LANES), BF16)], axis=1)
                pv = jnp.dot(p, v_ext, preferred_element_type=F32)
                acc = pv if acc is None else acc + pv
                off += kc
        for item in side_work:
            item()
        mx_front[...] = jnp.broadcast_to(jnp.max(run_max, axis=-1, keepdims=True), run_max.shape)
        o = acc[:, :LANES] / acc[:, LANES:]
        back_half = 1 - front_half
        if kind == "gqa":
            if back_half == 0:
                o_ref[...] = o.astype(BF16)
            else:
                low = lax.broadcasted_iota(jnp.int32, (1, LANES), 1) < HEAD_DIM
                o_ref[...] = jnp.where(low, o_ref[...], o.astype(BF16))
        elif back_half == 0:
            o0_scr[...] = o
        else:
            lp = lam_ref[...]
            lam = (jnp.exp(jnp.sum(lp[0:1, :] * lp[1:2, :], axis=-1, keepdims=True))
                   - jnp.exp(jnp.sum(lp[2:3, :] * lp[3:4, :], axis=-1, keepdims=True)) + lam_init)
            o_ref[...] = (_rms(o0_scr[...] - lam * o, g_ref[...]) * (1.0 - lam_init)).astype(BF16)

    @pl.when(t % 2 == 0)
    def _():
        step(0, s_a, mx_a, s_b, mx_b)

    @pl.when(t % 2 == 1)
    def _():
        step(1, s_b, mx_b, s_a, mx_a)


def _attention(kind, q, k, v, extra, n_batch, q_rows, q_row0, sources, lam_init=None, name="",
               conv=None):
    tq = min(ATTN_Q_ROWS, q_rows)
    n_q = q_rows // tq
    per_tile = 2 * BRANCH_W // LANES
    n_units = n_batch * n_q * per_tile
    kv_col = (lambda blk: 0) if kind == "gqa" else (lambda blk: blk)

    def decode(x):
        return x // (n_q * per_tile), (x // per_tile) % n_q, (x % per_tile) // 2

    front = lambda t: decode(jnp.minimum(t, n_units - 1))
    back = lambda t: decode(jnp.maximum(t - 1, 0))

    def q_idx(t):
        b, i, blk = front(t)
        return (q_row0 // tq + b * n_q + i, blk)

    def o_idx(t):
        b, i, blk = back(t)
        return (b * n_q + i, blk)

    def kv_idx(t, which, r):
        b, _, blk = which(t)
        return (r + b, kv_col(blk))

    in_specs = [pl.BlockSpec((tq, LANES), q_idx)]
    args = [q]
    for arr, which in ((k, front), (v, back)):
        for row0, n_keys in sources:
            in_specs.append(pl.BlockSpec(
                (n_keys, LANES), functools.partial(kv_idx, which=which, r=row0 // n_keys)))
            args.append(arr)
    total_keys = sum(nk for _, nk in sources)
    scratch = [pltpu.VMEM((tq, total_keys), F32), pltpu.VMEM((tq, total_keys), F32),
               pltpu.VMEM((tq, LANES), F32), pltpu.VMEM((tq, LANES), F32)]
    if kind == "diff":
        in_specs += [pl.BlockSpec(extra[0].shape, lambda t: (0, 0)),
                     pl.BlockSpec(extra[1].shape, lambda t: (0, 0))]
        args += list(extra)
        scratch.append(pltpu.VMEM((tq, LANES), F32))
    out_specs = [pl.BlockSpec((tq, LANES), o_idx)]
    out_shape = [jax.ShapeDtypeStruct((n_batch * q_rows, BRANCH_W), BF16)]
    conv_tiles_per_seq = None
    if conv is not None:
        y, conv_params = conv
        conv_tile = n_batch * q_rows // n_units
        assert q_row0 == 0 and conv_tile % CONV_ROWS == 0 and conv_tile >= CONV_HALO
        conv_tiles_per_seq = q_rows // conv_tile
        tile_idx = lambda t: jnp.minimum(t, n_units - 1)
        specs, y_args, conv_scratch = _conv_specs(y, conv_tile, tile_idx)
        in_specs += specs
        args += y_args + _conv_args(*conv_params)
        scratch += conv_scratch
        out_specs.append(pl.BlockSpec((conv_tile, BRANCH_W), lambda t: (tile_idx(t), 0)))
        out_shape.append(jax.ShapeDtypeStruct((n_batch * q_rows, BRANCH_W), BF16))
    outs = pl.pallas_call(
        functools.partial(_attn_kernel, kind=kind, key_counts=tuple(nk for _, nk in sources),
                          n_units=n_units, lam_init=lam_init,
                          conv_tiles_per_seq=conv_tiles_per_seq),
        grid=(n_units + 1,),
        in_specs=in_specs,
        out_specs=out_specs,
        out_shape=out_shape,
        scratch_shapes=scratch,
        compiler_params=_cparams(("arbitrary",)),
        name=name,
    )(*args)
    return outs[0] if conv is None else outs


def _conv_items(tile_in_seq, tiles_per_seq, prev_ref, cur_ref, next_ref, w_ref, b_ref, g_ref,
                beta_ref, o_ref, win_ref, stage_ref):
    hl, tm = CONV_HALO, cur_ref.shape[0]
    half, rc = CONV_W // 2, CONV_ROWS

    def stage():
        first, last = tile_in_seq == 0, tile_in_seq == tiles_per_seq - 1
        win_ref[0:hl, :] = jnp.where(first, 0.0, prev_ref[...])
        win_ref[hl:hl + tm, :] = cur_ref[...]
        win_ref[hl + tm:hl + tm + hl, :] = jnp.where(last, 0.0, next_ref[...])
        span = stage_ref.shape[1]
        for phase in range(SUBLANES):
            stage_ref[phase] = win_ref[phase:phase + span, :]

    def chunk(c):
        acc = jnp.zeros((rc // SUBLANES, SUBLANES, BRANCH_W), F32)
        for k in range(CONV_W):
            off = k + hl - half
            r0 = c * rc + off - off % SUBLANES
            xs = stage_ref[off % SUBLANES, r0:r0 + rc, :]
            acc = acc + xs.reshape(rc // SUBLANES, SUBLANES, BRANCH_W) * w_ref[k]
        y = acc.reshape(rc, BRANCH_W) + b_ref[...]
        mu = jnp.mean(y, axis=-1, keepdims=True)
        yc = y - mu
        var = jnp.mean(yc * yc, axis=-1, keepdims=True)
        yn = yc * lax.rsqrt(var + EPS) * g_ref[...] + beta_ref[...]
        o_ref[c * rc:(c + 1) * rc, :] = (yn * _sigmoid(yn)).astype(BF16)

    return [stage] + [functools.partial(chunk, c) for c in range(tm // rc)]


def _conv_specs(y, tile, tile_idx):
    hl = CONV_HALO
    r, n_halo = tile // hl, y.shape[0] // hl
    specs = [
        pl.BlockSpec((hl, BRANCH_W), lambda *a: (jnp.maximum(tile_idx(*a) * r - 1, 0), 0)),
        pl.BlockSpec((tile, BRANCH_W), lambda *a: (tile_idx(*a), 0)),
        pl.BlockSpec((hl, BRANCH_W),
                     lambda *a: (jnp.minimum((tile_idx(*a) + 1) * r, n_halo - 1), 0)),
        _const_spec((CONV_W, SUBLANES, BRANCH_W)),
        _const_spec((1, BRANCH_W)), _const_spec((1, BRANCH_W)), _const_spec((1, BRANCH_W)),
    ]
    scratch = [pltpu.VMEM((tile + 2 * hl, BRANCH_W), F32),
               pltpu.VMEM((SUBLANES, tile + 2 * hl - SUBLANES, BRANCH_W), F32)]
    return specs, [y, y, y], scratch


def _conv_args(w, b, g, beta):
    vec = lambda a: a.reshape(1, BRANCH_W)
    return [jnp.broadcast_to(w[:, None, :], (CONV_W, SUBLANES, BRANCH_W)), vec(b), vec(g),
            vec(beta)]


def _conv_kernel(*refs, tile0, tiles_per_seq):
    tile = tile0 + pl.program_id(0)
    for item in _conv_items(tile % tiles_per_seq, tiles_per_seq, *refs):
        item()


def _conv_module(y, w, b, g, beta, row0, n_rows, seq_len):
    tm = CONV_TILE
    tile0 = row0 // tm
    specs, y_args, scratch = _conv_specs(y, tm, lambda i: tile0 + i)
    return pl.pallas_call(
        functools.partial(_conv_kernel, tile0=tile0, tiles_per_seq=seq_len // tm),
        grid=(n_rows // tm,),
        in_specs=specs,
        out_specs=pl.BlockSpec((tm, BRANCH_W), lambda i: (i, 0)),
        out_shape=jax.ShapeDtypeStruct((n_rows, BRANCH_W), BF16),
        scratch_shapes=scratch,
        compiler_params=_cparams(("parallel",)),
        name="conv_module",
    )(*y_args, *_conv_args(w, b, g, beta))


def _dft_kernel(u_ref, cc_ref, sc_ref, flip_ref, t_ref, o_ref, z_ref, mid_ref, *, seq, tile):
    half = seq // 2
    n_fold = half // tile

    @pl.when(pl.program_id(1) == 0)
    def _():
        row = lax.broadcasted_iota(jnp.int32, (tile, 1), 0)
        for j in range(n_fold):
            src = (2 * n_fold - 1 - j) * tile
            rev = jnp.dot(flip_ref[...], u_ref[src:src + tile, :], preferred_element_type=F32)
            if j > 0:
                rev = jnp.where(row == 0, u_ref[src + tile:src + tile + 1, :].astype(F32), rev)
            x = u_ref[j * tile:(j + 1) * tile, :].astype(F32)
            plus, minus = (x + rev).astype(BF16), (x - rev).astype(BF16)
            for gidx in range(D_GROUPS):
                cols = slice(gidx * LANES, (gidx + 1) * LANES)
                z_ref[j * tile:(j + 1) * tile, cols] = jnp.dot(
                    plus[:, cols], cc_ref[...], preferred_element_type=F32).astype(BF16)
                z_ref[half + j * tile:half + (j + 1) * tile, cols] = jnp.dot(
                    minus[:, cols], sc_ref[...], preferred_element_type=F32).astype(BF16)
        for gidx in range(D_GROUPS):
            cols = slice(gidx * LANES, (gidx + 1) * LANES)
            mid_ref[:, cols] = jnp.dot(u_ref[half:half + mid_ref.shape[0], cols], cc_ref[...],
                                       preferred_element_type=F32)

    k_odd = lax.broadcasted_iota(jnp.int32, (tile, 1), 0) % 2
    sign = 1.0 - 2.0 * k_odd.astype(F32)
    o_ref[...] = (jnp.dot(t_ref[...], z_ref[...], preferred_element_type=F32)
                  + sign * mid_ref[0:1, :]).astype(BF16)


def _dft_tables(seq):
    def cs_rows(rows, cols, period):
        ang = ((rows[:, None] * cols[None, :]) % period).astype(F32) * (2.0 * math.pi / period)
        return jnp.cos(ang), jnp.sin(ang)

    def cs(n, cols):
        split = 64
        if n <= split or n % split:
            return cs_rows(jnp.arange(n, dtype=jnp.int32), cols, n)
        ac, as_ = cs_rows(jnp.arange(n // split, dtype=jnp.int32), cols, n // split)
        bc, bs = cs_rows(jnp.arange(split, dtype=jnp.int32), cols, n)
        cos = ac[:, None, :] * bc[None, :, :] - as_[:, None, :] * bs[None, :, :]
        sin = as_[:, None, :] * bc[None, :, :] + ac[:, None, :] * bs[None, :, :]
        return cos.reshape(n, cols.shape[0]), sin.reshape(n, cols.shape[0])
    half = seq // 2
    col = jnp.arange(seq, dtype=jnp.int32)
    ct, st = cs(seq, col % half)
    tmat = jnp.where(col[None, :] < half, ct, -st).astype(BF16)
    cc, sc = cs(D_GROUP_CH, jnp.arange(D_GROUP_CH, dtype=jnp.int32))
    norm = 1.0 / math.sqrt(seq * D_GROUP_CH)
    tile = min(DFT_ROW_TILE, half)
    i = jnp.arange(tile, dtype=jnp.int32)
    flip = ((i[:, None] + i[None, :] == tile) & (i[:, None] > 0)).astype(BF16)
    return (cc * norm).astype(BF16), (sc * norm).astype(BF16), flip, tmat


def _fourier_mix(u, tables, n_batch, seq, row0, name):
    cc, sc, flip, tmat = tables
    tr = flip.shape[0]
    n_r = seq // tr
    mid_rows = 2 * SUBLANES
    return pl.pallas_call(
        functools.partial(_dft_kernel, seq=seq, tile=tr),
        grid=(n_batch, n_r),
        in_specs=[
            pl.BlockSpec((seq, BRANCH_W), lambda b, r: (row0 // seq + b, 0)),
            pl.BlockSpec((LANES, LANES), lambda b, r: (0, 0)),
            pl.BlockSpec((LANES, LANES), lambda b, r: (0, 0)),
            pl.BlockSpec((tr, tr), lambda b, r: (0, 0)),
            pl.BlockSpec((tr, seq), lambda b, r: (r, 0)),
        ],
        out_specs=pl.BlockSpec((tr, BRANCH_W), lambda b, r: (b * n_r + r, 0)),
        out_shape=jax.ShapeDtypeStruct((n_batch * seq, BRANCH_W), BF16),
        scratch_shapes=[pltpu.VMEM((seq, BRANCH_W), BF16), pltpu.VMEM((mid_rows, BRANCH_W), F32)],
        compiler_params=_cparams(("parallel", "arbitrary")),
        name=name,
    )(u, cc, sc, flip, tmat)


def _merge_kernel(*refs, has_ctx, lat_tiles):
    refs = list(refs)
    n_y = sum(2 if c else 1 for c in has_ctx)
    y_refs, (gt_ref, h_ref, mod_ref, wba_ref, wb_ref, wo_ref, o_ref) = refs[:n_y], refs[n_y:]
    is_lat = pl.program_id(0) < lat_tiles
    acc = None
    for j, split in enumerate(has_ctx):
        y = y_refs.pop(0)[...]
        if split:
            y = jnp.where(is_lat, y, y_refs.pop(0)[...])
        w = wba_ref[...] if j == 0 else wb_ref[j]
        term = gt_ref[:, j * D_MODEL:(j + 1) * D_MODEL].astype(F32) * jnp.dot(
            y, w, preferred_element_type=F32)
        acc = term if acc is None else acc + term
    y = jnp.dot(acc.astype(BF16), wo_ref[...], preferred_element_type=F32)
    o_ref[...] = h_ref[...] + mod_ref[2:3, :] * y


def _merge(ys, gates, h, mods, layer, wb_a, wb, wo, n_tiles, lat_tiles, tiles_per_batch, ctx_row):
    tm = TOKEN_TILE
    row = lambda wd: pl.BlockSpec((tm, wd), lambda i: (i, 0))
    y_specs, y_args, has_ctx = [], [], []
    for y in ys:
        split = isinstance(y, tuple)
        has_ctx.append(split)
        if split:
            y_specs += [pl.BlockSpec((tm, BRANCH_W), lambda i: (jnp.minimum(i, lat_tiles - 1), 0)),
                        pl.BlockSpec((tm, BRANCH_W), lambda i: (jnp.maximum(i - lat_tiles, 0), 0))]
            y_args += list(y)
        else:
            y_specs.append(row(BRANCH_W))
            y_args.append(y)
    return pl.pallas_call(
        functools.partial(_merge_kernel, has_ctx=tuple(has_ctx), lat_tiles=lat_tiles),
        grid=(n_tiles,),
        in_specs=y_specs + [
            row(N_BRANCH * D_MODEL), row(D_MODEL),
            _mod_spec(layer, 1, lat_tiles, tiles_per_batch, ctx_row),
            _const_spec((BRANCH_W, D_MODEL)),
            _const_spec((N_BRANCH, BRANCH_W, D_MODEL), (layer,)),
            _const_spec((D_MODEL, D_MODEL), (layer,)),
        ],
        out_specs=row(D_MODEL),
        out_shape=jax.ShapeDtypeStruct((n_tiles * tm, D_MODEL), F32),
        compiler_params=_cparams(("parallel",)),
        name=f"merge_l{layer}",
    )(*y_args, gates, h, mods, wb_a, wb, wo)


def _rope_tables(seq, pad_rows):
    pos = jnp.arange(seq, dtype=jnp.int32)
    row = (pos // GRID_W).astype(F32)
    col = (pos % GRID_W).astype(F32)
    half = HEAD_DIM // 2
    inv = ROPE_BASE ** (-jnp.arange(0, half, 2, dtype=F32) / half)
    ang_r, ang_c = row[:, None] * inv, col[:, None] * inv
    ang = jnp.concatenate([ang_r, ang_r, ang_c, ang_c], axis=-1)
    cos, sin = jnp.cos(ang), jnp.sin(ang)
    lane = jnp.arange(HEAD_DIM)
    sin_s = jnp.where((lane & 16) == 0, -sin, sin)
    cos = jnp.concatenate([cos, jnp.ones((pad_rows, HEAD_DIM), F32)], axis=0)
    sin_s = jnp.concatenate([sin_s, jnp.zeros((pad_rows, HEAD_DIM), F32)], axis=0)
    return jnp.tile(cos, (1, 2)), jnp.tile(sin_s, (1, 2))


def _paired_query_weight(w_in):
    aq = w_in[:, :BRANCH_W].reshape(D_MODEL, A_KV_HEADS, A_GROUP, HEAD_DIM)
    return jnp.transpose(aq, (0, 2, 1, 3)).reshape(D_MODEL, BRANCH_W).astype(BF16)


def kernel(x, c, ctx, c_ctx, ada_w, ada_b, norm_g, ffn_w1, ffn_w3, ffn_w2, w_in, qk_norm_a,
           conv_w, conv_b, conv_ln_g, conv_ln_b, diff_lam, diff_subln_g, w_branch, w_out, final_g):
    n_batch, seq, _ = x.shape
    ctx_len = ctx.shape[1]
    depth = ada_w.shape[0]
    tm = TOKEN_TILE
    lat_rows, ctx_rows = n_batch * seq, n_batch * ctx_len
    assert seq % tm == 0 and ctx_rows % tm == 0 and ctx_len % CONV_TILE == 0
    assert n_batch < COND_ROWS and seq % GRID_W == 0
    lat_tiles, all_tiles = lat_rows // tm, (lat_rows + ctx_rows) // tm
    tiles_per_batch = seq // tm
    ctx_row = n_batch

    cond = jnp.concatenate(
        [c, c_ctx[None, :], jnp.zeros((COND_ROWS - n_batch - 1, D_MODEL), F32)], axis=0)
    mods = _adaln(cond, ada_w, ada_b).reshape(depth, COND_ROWS, 3, 3, D_MODEL)

    cos_t, sin_t = _rope_tables(seq, tm)
    lane = jnp.arange(LANES)
    seg = (lane[:, None] // HEAD_DIM == lane[None, :] // HEAD_DIM).astype(BF16)
    dft_lat = _dft_tables(seq)
    dft_ctx = _dft_tables(ctx_len)
    tile_args = (lat_tiles, tiles_per_batch, ctx_row)

    h, h_ctx = x.reshape(lat_rows, D_MODEL), ctx.reshape(ctx_rows, D_MODEL)
    w1_bf, w3_bf, w2_bf = ffn_w1.astype(BF16), ffn_w3.astype(BF16), ffn_w2.astype(BF16)
    w_in_bf, wb_bf, wo_bf = w_in.astype(BF16), w_branch.astype(BF16), w_out.astype(BF16)
    for l in range(depth):
        update_ctx = l < depth - 1
        lam_init = 0.8 - 0.6 * math.exp(-0.3 * l)
        out_tiles = all_tiles if update_ctx else lat_tiles
        wb_a = wb_bf[l, 0].reshape(A_KV_HEADS, A_GROUP, HEAD_DIM, D_MODEL)
        wb_a = jnp.transpose(wb_a, (1, 0, 2, 3)).reshape(BRANCH_W, D_MODEL)
        qkg = jnp.tile(qk_norm_a[l], (1, 2))

        h = _half_ffn(h, mods, l, 0, norm_g[l, 0], w1_bf, w3_bf, w2_bf, (l, 0),
                      all_tiles, *tile_args, h_ctx=h_ctx)
        h_ctx = None
        qa, ka, va, yg, qc, kc, vc, du, gates = _in_proj(
            h, mods, l, norm_g[l, 1], w_in_bf, _paired_query_weight(w_in[l]), cos_t, sin_t,
            seg, qkg, all_tiles, *tile_args)

        lat_src = ((0, seq), (lat_rows, ctx_len))
        ctx_src = ((lat_rows, ctx_len),)
        diff_extra = (diff_lam[l], diff_subln_g[l].reshape(1, LANES))
        conv_params = (conv_w[l], conv_b[l], conv_ln_g[l], conv_ln_b[l])
        y_a, y_b = _attention("gqa", qa, ka, va, None, n_batch, seq, 0, lat_src,
                              name=f"gqa_l{l}", conv=(yg, conv_params))
        y_c = _attention("diff", qc, kc, vc, diff_extra, n_batch, seq, 0, lat_src,
                         lam_init=lam_init, name=f"diff_l{l}")
        y_d = _fourier_mix(du, dft_lat, n_batch, seq, 0, f"dft_l{l}")
        if update_ctx:
            y_b = (y_b, _conv_module(yg, *conv_params, lat_rows, ctx_rows, ctx_len))
            y_a = (y_a, _attention("gqa", qa, ka, va, None, n_batch, ctx_len, lat_rows, ctx_src,
                                   name=f"gqa_ctx_l{l}"))
            y_c = (y_c, _attention("diff", qc, kc, vc, diff_extra, n_batch, ctx_len, lat_rows,
                                   ctx_src, lam_init=lam_init, name=f"diff_ctx_l{l}"))
            y_d = (y_d, _fourier_mix(du, dft_ctx, n_batch, ctx_len, lat_rows, f"dft_ctx_l{l}"))
        h = _merge((y_a, y_b, y_c, y_d), gates, h, mods, l, wb_a, wb_bf, wo_bf,
                   out_tiles, *tile_args)
        h = _half_ffn(h, mods, l, 2, norm_g[l, 2], w1_bf, w3_bf, w2_bf, (l, 1),
                      out_tiles, *tile_args, final_g=None if update_ctx else final_g)
    return h.reshape(n_batch, seq, D_MODEL)
```

```python
import functools
import math

import jax
import jax.numpy as jnp
from jax import lax
from jax.experimental import pallas as pl
from jax.experimental.pallas import tpu as pltpu

F32 = jnp.float32
BF16 = jnp.bfloat16

D_MODEL = 1024
GRID_W = 64
HEAD_DIM = 64
ROPE_BASE = 10000.0
EPS = 1e-6
N_MOD = 9
N_BRANCH = 4
BRANCH_W = D_MODEL // 2
A_Q_HEADS = BRANCH_W // HEAD_DIM
A_KV_HEADS = 2
A_GROUP = A_Q_HEADS // A_KV_HEADS
CONV_W = 31
C_HEADS = BRANCH_W // (2 * HEAD_DIM)
D_GROUPS = 4
D_GROUP_CH = BRANCH_W // D_GROUPS
D_FF = 256 * ((8 * D_MODEL // 3 + 255) // 256)
ATTN_SCALE = HEAD_DIM ** -0.5

LANES = 128
SUBLANES = 8
COND_ROWS = 16
TOKEN_TILE = 512
FF_CHUNK = 256
CONV_TILE = 256
CONV_HALO = 16
CONV_ROWS = 16
ATTN_Q_ROWS = 1024
ATTN_KEY_CHUNK = 1024
Q_SCALE = ATTN_SCALE * math.log2(math.e)
DFT_ROW_TILE = 512
VMEM_LIMIT = 56 * 1024 * 1024


def _cparams(sem):
    return pltpu.CompilerParams(dimension_semantics=sem, vmem_limit_bytes=VMEM_LIMIT)


def _sigmoid(x):
    return 1.0 / (1.0 + jnp.exp(-x))


def _const_spec(shape, lead=()):
    nd = len(shape)
    return pl.BlockSpec((None,) * len(lead) + tuple(shape), lambda *_: tuple(lead) + (0,) * nd,
                        pipeline_mode=pl.Buffered(1))


def _ada_kernel(cond_ref, w_ref, b_ref, o_ref):
    cnd = cond_ref[...]
    s = cnd * _sigmoid(cnd)
    o_ref[...] = jnp.dot(s, w_ref[...], preferred_element_type=F32,
                         precision=lax.Precision.HIGHEST) + b_ref[...]


def _adaln(cond, ada_w, ada_b):
    depth = ada_w.shape[0]
    n_out = N_MOD * D_MODEL
    col = D_MODEL
    return pl.pallas_call(
        _ada_kernel,
        grid=(depth, n_out // col),
        in_specs=[
            pl.BlockSpec((COND_ROWS, D_MODEL), lambda l, j: (0, 0)),
            pl.BlockSpec((None, D_MODEL, col), lambda l, j: (l, 0, j)),
            pl.BlockSpec((None, 1, col), lambda l, j: (l, 0, j)),
        ],
        out_specs=pl.BlockSpec((None, COND_ROWS, col), lambda l, j: (l, 0, j)),
        out_shape=jax.ShapeDtypeStruct((depth, COND_ROWS, n_out), F32),
        compiler_params=_cparams(("arbitrary", "arbitrary")),
        name="adaln",
    )(cond, ada_w, ada_b.reshape(depth, 1, n_out))


def _rms(x, g):
    return x * lax.rsqrt(jnp.mean(x * x, axis=-1, keepdims=True) + EPS) * g


def _mod_spec(layer, sub, lat_tiles, tiles_per_batch, ctx_row):
    def idx(i):
        row = jnp.where(i < lat_tiles, i // tiles_per_batch, ctx_row)
        return (layer, row, sub, 0, 0)
    return pl.BlockSpec((None, None, None, 3, D_MODEL), idx)


def _ffn_kernel(*refs, final, lat_tiles, split):
    refs = list(refs)
    h_ref = refs.pop(0)
    ctx_ref = refs.pop(0) if split else None
    mod_ref, g_ref, w1_ref, w3_ref, w2_ref = refs[:5]
    fg_ref = refs[5] if final else None
    o_ref = refs[-1]
    x = h_ref[...]
    if split:
        x = jnp.where(pl.program_id(0) < lat_tiles, x, ctx_ref[...])
    shift, scale, gate = mod_ref[0:1, :], mod_ref[1:2, :], mod_ref[2:3, :]
    n = (_rms(x, g_ref[...]) * (1.0 + scale) + shift).astype(BF16)
    acc = jnp.zeros(x.shape, F32)
    c = FF_CHUNK
    for j in range(D_FF // c):
        a = jnp.dot(n, w1_ref[:, c * j:c * (j + 1)], preferred_element_type=F32)
        b = jnp.dot(n, w3_ref[:, c * j:c * (j + 1)], preferred_element_type=F32)
        p = (a * _sigmoid(a) * b).astype(BF16)
        acc = acc + jnp.dot(p, w2_ref[c * j:c * (j + 1), :], preferred_element_type=F32)
    out = x + 0.5 * gate * acc
    if final:
        out = _rms(out, fg_ref[...])
    o_ref[...] = out


def _half_ffn(h, mods, layer, sub, g, w1, w3, w2, wsel, n_tiles, lat_tiles, tiles_per_batch, ctx_row,
              final_g=None, h_ctx=None):
    tm = TOKEN_TILE
    final = final_g is not None
    split = h_ctx is not None
    if split:
        in_specs = [pl.BlockSpec((tm, D_MODEL), lambda i: (jnp.minimum(i, lat_tiles - 1), 0)),
                    pl.BlockSpec((tm, D_MODEL), lambda i: (jnp.maximum(i - lat_tiles, 0), 0))]
        args = [h, h_ctx]
    else:
        in_specs = [pl.BlockSpec((tm, D_MODEL), lambda i: (i, 0))]
        args = [h]
    in_specs += [
        _mod_spec(layer, sub, lat_tiles, tiles_per_batch, ctx_row),
        _const_spec((1, D_MODEL)),
        _const_spec((D_MODEL, D_FF), wsel),
        _const_spec((D_MODEL, D_FF), wsel),
        _const_spec((D_FF, D_MODEL), wsel),
    ]
    args += [mods, g.reshape(1, D_MODEL), w1, w3, w2]
    if final:
        in_specs.append(_const_spec((1, D_MODEL)))
        args.append(final_g.reshape(1, D_MODEL))
    return pl.pallas_call(
        functools.partial(_ffn_kernel, final=final, lat_tiles=lat_tiles, split=split),
        grid=(n_tiles,),
        in_specs=in_specs,
        out_specs=pl.BlockSpec((tm, D_MODEL), lambda i: (i, 0)),
        out_shape=jax.ShapeDtypeStruct((n_tiles * tm, D_MODEL), F32),
        compiler_params=_cparams(("parallel",)),
        name=f"ffn_l{layer}_s{sub}",
    )(*args)


_IN_SEGS = (("aq", 512), ("ak", 128), ("av", 128), ("ba", 512), ("bg", 512),
            ("cq", 512), ("ck", 512), ("cv", 512), ("du", 512), ("gl", 4096))
_IN_OFF = {}
_o = 0
for _n, _s in _IN_SEGS:
    _IN_OFF[_n] = (_o, _o + _s)
    _o += _s
IN_COLS = _o


def _rope_blocks(x, cos, sin_s, low16):
    outs = []
    for j in range(x.shape[1] // LANES):
        xb = x[:, j * LANES:(j + 1) * LANES]
        partner = jnp.where(low16, pltpu.roll(xb, LANES - 16, 1), pltpu.roll(xb, 16, 1))
        outs.append(xb * cos + partner * sin_s)
    return outs


def _in_kernel(h_ref, mod_ref, g_ref, w_ref, wq_ref, cos_ref, sin_ref, seg_ref, qkg_ref,
               qa_ref, ka_ref, va_ref, yg_ref, qc_ref, kc_ref, vc_ref, du_ref, gt_ref):
    x = h_ref[...]
    shift, scale = mod_ref[0:1, :], mod_ref[1:2, :]
    n = (_rms(x, g_ref[...]) * (1.0 + scale) + shift).astype(BF16)

    def proj(name, lo=None, hi=None):
        if name == "aq":
            return jnp.dot(n, wq_ref[...], preferred_element_type=F32)
        a, b = _IN_OFF[name]
        if lo is not None:
            a, b = a + lo, a + hi
        return jnp.dot(n, w_ref[:, a:b], preferred_element_type=F32)

    cos, sin_s = cos_ref[...], sin_ref[...]
    lane = lax.broadcasted_iota(jnp.int32, (1, LANES), 1)
    low16 = (lane & 16) == 0
    seg = seg_ref[...]

    def head_norm(blocks, gains):
        rows = blocks[0].shape[0]
        sq = jnp.concatenate([zb * zb for zb in blocks], axis=0).astype(BF16)
        ss = jnp.dot(sq, seg, preferred_element_type=F32)
        return [zb * lax.rsqrt(ss[j * rows:(j + 1) * rows, :] * (1.0 / HEAD_DIM) + EPS) * g
                for j, (zb, g) in enumerate(zip(blocks, gains))]

    zq, zk = proj("aq"), proj("ak")
    nq = BRANCH_W // LANES
    normed = head_norm([zq[:, j * LANES:(j + 1) * LANES] for j in range(nq)] + [zk],
                       [qkg_ref[0:1, :]] * nq + [qkg_ref[1:2, :]])
    q = jnp.concatenate(normed[:nq], axis=1)
    qa_ref[...] = jnp.concatenate(
        [b * Q_SCALE for b in _rope_blocks(q, cos, sin_s, low16)], axis=1).astype(BF16)
    ka_ref[...] = _rope_blocks(normed[nq], cos, sin_s, low16)[0].astype(BF16)
    va_ref[...] = proj("av").astype(BF16)
    yg_ref[...] = proj("ba") * _sigmoid(proj("bg"))
    qc_ref[...] = jnp.concatenate(
        [b * Q_SCALE for b in _rope_blocks(proj("cq"), cos, sin_s, low16)], axis=1).astype(BF16)
    kc_ref[...] = jnp.concatenate(_rope_blocks(proj("ck"), cos, sin_s, low16), axis=1).astype(BF16)
    vc_ref[...] = proj("cv").astype(BF16)
    du_ref[...] = proj("du").astype(BF16)
    for j in range(N_BRANCH):
        gt_ref[:, j * D_MODEL:(j + 1) * D_MODEL] = _sigmoid(
            proj("gl", j * D_MODEL, (j + 1) * D_MODEL)).astype(BF16)


def _in_proj(h, mods, layer, g, w, wq, cos_t, sin_t, seg, qkg, n_tiles, lat_tiles,
             tiles_per_batch, ctx_row):
    tm = TOKEN_TILE
    t = n_tiles * tm

    def rope_idx(i):
        return (jnp.where(i < lat_tiles, i % tiles_per_batch, tiles_per_batch), 0)

    widths = (512, 128, 128, 512, 512, 512, 512, 512, 4096)
    dtypes = (BF16, BF16, BF16, F32, BF16, BF16, BF16, BF16, BF16)
    return pl.pallas_call(
        _in_kernel,
        grid=(n_tiles,),
        in_specs=[
            pl.BlockSpec((tm, D_MODEL), lambda i: (i, 0)),
            _mod_spec(layer, 1, lat_tiles, tiles_per_batch, ctx_row),
            _const_spec((1, D_MODEL)),
            _const_spec((D_MODEL, IN_COLS), (layer,)),
            _const_spec((D_MODEL, BRANCH_W)),
            pl.BlockSpec((tm, LANES), rope_idx),
            pl.BlockSpec((tm, LANES), rope_idx),
            _const_spec((LANES, LANES)),
            _const_spec((2, LANES)),
        ],
        out_specs=[pl.BlockSpec((tm, wd), lambda i: (i, 0)) for wd in widths],
        out_shape=[jax.ShapeDtypeStruct((t, wd), dt) for wd, dt in zip(widths, dtypes)],
        compiler_params=_cparams(("parallel",)),
        name=f"in_proj_l{layer}",
    )(h, mods, g.reshape(1, D_MODEL), w, wq, cos_t, sin_t, seg, qkg)


def _attn_kernel(*refs, kind, key_counts, n_units, lam_init, conv_tiles_per_seq):
    n_src = len(key_counts)
    q_ref = refs[0]
    k_refs, v_refs = refs[1:1 + n_src], refs[1 + n_src:1 + 2 * n_src]
    rest = list(refs[1 + 2 * n_src:])
    if kind == "diff":
        lam_ref, g_ref = rest.pop(0), rest.pop(0)
    conv_in = [rest.pop(0) for _ in range(7)] if conv_tiles_per_seq else None
    o_ref = rest.pop(0)
    conv_out = rest.pop(0) if conv_tiles_per_seq else None
    s_a, s_b, mx_a, mx_b = rest[:4]
    if kind == "diff":
        o0_scr = rest[4]
    conv_scratch = rest[-2:] if conv_tiles_per_seq else None
    t = pl.program_id(0)

    @pl.when(t == 0)
    def _():
        s_b[...] = jnp.zeros(s_b.shape, F32)
        mx_b[...] = jnp.zeros(mx_b.shape, F32)
        if kind == "gqa":
            o_ref[...] = jnp.zeros(o_ref.shape, BF16)
        else:
            o0_scr[...] = jnp.zeros(o0_scr.shape, F32)

    def step(front_half, s_front, mx_front, s_back, mx_back):
        lane_half = lax.broadcasted_iota(jnp.int32, (1, LANES), 1) // HEAD_DIM
        qp = jnp.where(lane_half == front_half, q_ref[...].astype(F32), 0.0).astype(BF16)
        m_back = mx_back[:, 0:1]
        run_max, acc, off = None, None, 0
        side_work = []
        if conv_tiles_per_seq:
            conv_tile = jnp.minimum(t, n_units - 1)
            side_work = _conv_items(conv_tile % conv_tiles_per_seq, conv_tiles_per_seq,
                                    *conv_in, conv_out, *conv_scratch)
        for k_ref, v_ref, n_keys in zip(k_refs, v_refs, key_counts):
            for c0 in range(0, n_keys, ATTN_KEY_CHUNK):
                if side_work:
                    side_work.pop(0)()
                kc = min(ATTN_KEY_CHUNK, n_keys - c0)
                s = lax.dot_general(qp, k_ref[c0:c0 + kc, :], (((1,), (1,)), ((), ())),
                                    preferred_element_type=F32)
                s_front[:, off:off + kc] = s
                part = functools.reduce(
                    jnp.maximum, [s[:, j * LANES:(j + 1) * LANES] for j in range(kc // LANES)])
                run_max = part if run_max is None else jnp.maximum(run_max, part)
                p = jnp.exp2(s_back[:, off:off + kc] - m_back).astype(BF16)
                v_ext = jnp.concatenate(
                    [v_ref[c0:c0 + kc, :], jnp.ones((kc, LANES), BF16)], axis=1)
                pv = jnp.dot(p, v_ext, preferred_element_type=F32)
                acc = pv if acc is None else acc + pv
                off += kc
        for item in side_work:
            item()
        mx_front[...] = jnp.broadcast_to(jnp.max(run_max, axis=-1, keepdims=True), run_max.shape)
        o = acc[:, :LANES] / acc[:, LANES:]
        back_half = 1 - front_half
        if kind == "gqa":
            if back_half == 0:
                o_ref[...] = o.astype(BF16)
            else:
                low = lax.broadcasted_iota(jnp.int32, (1, LANES), 1) < HEAD_DIM
                o_ref[...] = jnp.where(low, o_ref[...], o.astype(BF16))
        elif back_half == 0:
            o0_scr[...] = o
        else:
            lp = lam_ref[...]
            lam = (jnp.exp(jnp.sum(lp[0:1, :] * lp[1:2, :], axis=-1, keepdims=True))
                   - jnp.exp(jnp.sum(lp[2:3, :] * lp[3:4, :], axis=-1, keepdims=True)) + lam_init)
            o_ref[...] = (_rms(o0_scr[...] - lam * o, g_ref[...]) * (1.0 - lam_init)).astype(BF16)

    @pl.when(t % 2 == 0)
    def _():
        step(0, s_a, mx_a, s_b, mx_b)

    @pl.when(t % 2 == 1)
    def _():
        step(1, s_b, mx_b, s_a, mx_a)


def _attention(kind, q, k, v, extra, n_batch, q_rows, q_row0, sources, lam_init=None, name="",
               conv=None):
    tq = min(ATTN_Q_ROWS, q_rows)
    n_q = q_rows // tq
    per_tile = 2 * BRANCH_W // LANES
    n_units = n_batch * n_q * per_tile
    kv_col = (lambda blk: 0) if kind == "gqa" else (lambda blk: blk)

    def decode(x):
        return x // (n_q * per_tile), (x // per_tile) % n_q, (x % per_tile) // 2

    front = lambda t: decode(jnp.minimum(t, n_units - 1))
    back = lambda t: decode(jnp.maximum(t - 1, 0))

    def q_idx(t):
        b, i, blk = front(t)
        return (q_row0 // tq + b * n_q + i, blk)

    def o_idx(t):
        b, i, blk = back(t)
        return (b * n_q + i, blk)

    def kv_idx(t, which, r):
        b, _, blk = which(t)
        return (r + b, kv_col(blk))

    in_specs = [pl.BlockSpec((tq, LANES), q_idx)]
    args = [q]
    for arr, which in ((k, front), (v, back)):
        for row0, n_keys in sources:
            in_specs.append(pl.BlockSpec(
                (n_keys, LANES), functools.partial(kv_idx, which=which, r=row0 // n_keys)))
            args.append(arr)
    total_keys = sum(nk for _, nk in sources)
    scratch = [pltpu.VMEM((tq, total_keys), F32), pltpu.VMEM((tq, total_keys), F32),
               pltpu.VMEM((tq, LANES), F32), pltpu.VMEM((tq, LANES), F32)]
    if kind == "diff":
        in_specs += [pl.BlockSpec(extra[0].shape, lambda t: (0, 0)),
                     pl.BlockSpec(extra[1].shape, lambda t: (0, 0))]
        args += list(extra)
        scratch.append(pltpu.VMEM((tq, LANES), F32))
    out_specs = [pl.BlockSpec((tq, LANES), o_idx)]
    out_shape = [jax.ShapeDtypeStruct((n_batch * q_rows, BRANCH_W), BF16)]
    conv_tiles_per_seq = None
    if conv is not None:
        y, conv_params = conv
        conv_tile = n_batch * q_rows // n_units
        assert q_row0 == 0 and conv_tile % CONV_ROWS == 0 and conv_tile >= CONV_HALO
        conv_tiles_per_seq = q_rows // conv_tile
        tile_idx = lambda t: jnp.minimum(t, n_units - 1)
        specs, y_args, conv_scratch = _conv_specs(y, conv_tile, tile_idx)
        in_specs += specs
        args += y_args + _conv_args(*conv_params)
        scratch += conv_scratch
        out_specs.append(pl.BlockSpec((conv_tile, BRANCH_W), lambda t: (tile_idx(t), 0)))
        out_shape.append(jax.ShapeDtypeStruct((n_batch * q_rows, BRANCH_W), BF16))
    outs = pl.pallas_call(
        functools.partial(_attn_kernel, kind=kind, key_counts=tuple(nk for _, nk in sources),
                          n_units=n_units, lam_init=lam_init,
                          conv_tiles_per_seq=conv_tiles_per_seq),
        grid=(n_units + 1,),
        in_specs=in_specs,
        out_specs=out_specs,
        out_shape=out_shape,
        scratch_shapes=scratch,
        compiler_params=_cparams(("arbitrary",)),
        name=name,
    )(*args)
    return outs[0] if conv is None else outs


def _conv_items(tile_in_seq, tiles_per_seq, prev_ref, cur_ref, next_ref, w_ref, b_ref, g_ref,
                beta_ref, o_ref, win_ref, stage_ref):
    hl, tm = CONV_HALO, cur_ref.shape[0]
    half, rc = CONV_W // 2, CONV_ROWS

    def stage():
        first, last = tile_in_seq == 0, tile_in_seq == tiles_per_seq - 1
        win_ref[0:hl, :] = jnp.where(first, 0.0, prev_ref[...])
        win_ref[hl:hl + tm, :] = cur_ref[...]
        win_ref[hl + tm:hl + tm + hl, :] = jnp.where(last, 0.0, next_ref[...])
        span = stage_ref.shape[1]
        for phase in range(SUBLANES):
            stage_ref[phase] = win_ref[phase:phase + span, :]

    def chunk(c):
        acc = jnp.zeros((rc // SUBLANES, SUBLANES, BRANCH_W), F32)
        for k in range(CONV_W):
            off = k + hl - half
            r0 = c * rc + off - off % SUBLANES
            xs = stage_ref[off % SUBLANES, r0:r0 + rc, :]
            acc = acc + xs.reshape(rc // SUBLANES, SUBLANES, BRANCH_W) * w_ref[k]
        y = acc.reshape(rc, BRANCH_W) + b_ref[...]
        mu = jnp.mean(y, axis=-1, keepdims=True)
        yc = y - mu
        var = jnp.mean(yc * yc, axis=-1, keepdims=True)
        yn = yc * lax.rsqrt(var + EPS) * g_ref[...] + beta_ref[...]
        o_ref[c * rc:(c + 1) * rc, :] = (yn * _sigmoid(yn)).astype(BF16)

    return [stage] + [functools.partial(chunk, c) for c in range(tm // rc)]


def _conv_specs(y, tile, tile_idx):
    hl = CONV_HALO
    r, n_halo = tile // hl, y.shape[0] // hl
    specs = [
        pl.BlockSpec((hl, BRANCH_W), lambda *a: (jnp.maximum(tile_idx(*a) * r - 1, 0), 0)),
        pl.BlockSpec((tile, BRANCH_W), lambda *a: (tile_idx(*a), 0)),
        pl.BlockSpec((hl, BRANCH_W),
                     lambda *a: (jnp.minimum((tile_idx(*a) + 1) * r, n_halo - 1), 0)),
        _const_spec((CONV_W, SUBLANES, BRANCH_W)),
        _const_spec((1, BRANCH_W)), _const_spec((1, BRANCH_W)), _const_spec((1, BRANCH_W)),
    ]
    scratch = [pltpu.VMEM((tile + 2 * hl, BRANCH_W), F32),
               pltpu.VMEM((SUBLANES, tile + 2 * hl - SUBLANES, BRANCH_W), F32)]
    return specs, [y, y, y], scratch


def _conv_args(w, b, g, beta):
    vec = lambda a: a.reshape(1, BRANCH_W)
    return [jnp.broadcast_to(w[:, None, :], (CONV_W, SUBLANES, BRANCH_W)), vec(b), vec(g),
            vec(beta)]


def _conv_kernel(*refs, tile0, tiles_per_seq):
    tile = tile0 + pl.program_id(0)
    for item in _conv_items(tile % tiles_per_seq, tiles_per_seq, *refs):
        item()


def _conv_module(y, w, b, g, beta, row0, n_rows, seq_len):
    tm = CONV_TILE
    tile0 = row0 // tm
    specs, y_args, scratch = _conv_specs(y, tm, lambda i: tile0 + i)
    return pl.pallas_call(
        functools.partial(_conv_kernel, tile0=tile0, tiles_per_seq=seq_len // tm),
        grid=(n_rows // tm,),
        in_specs=specs,
        out_specs=pl.BlockSpec((tm, BRANCH_W), lambda i: (i, 0)),
        out_shape=jax.ShapeDtypeStruct((n_rows, BRANCH_W), BF16),
        scratch_shapes=scratch,
        compiler_params=_cparams(("parallel",)),
        name="conv_module",
    )(*y_args, *_conv_args(w, b, g, beta))


def _dft_kernel(u_ref, cc_ref, sc_ref, flip_ref, t_ref, o_ref, z_ref, mid_ref, *, seq, tile):
    half = seq // 2
    n_fold = half // tile

    @pl.when(pl.program_id(1) == 0)
    def _():
        row = lax.broadcasted_iota(jnp.int32, (tile, 1), 0)
        for j in range(n_fold):
            src = (2 * n_fold - 1 - j) * tile
            rev = jnp.dot(flip_ref[...], u_ref[src:src + tile, :], preferred_element_type=F32)
            if j > 0:
                rev = jnp.where(row == 0, u_ref[src + tile:src + tile + 1, :].astype(F32), rev)
            x = u_ref[j * tile:(j + 1) * tile, :].astype(F32)
            plus, minus = (x + rev).astype(BF16), (x - rev).astype(BF16)
            for gidx in range(D_GROUPS):
                cols = slice(gidx * LANES, (gidx + 1) * LANES)
                z_ref[j * tile:(j + 1) * tile, cols] = jnp.dot(
                    plus[:, cols], cc_ref[...], preferred_element_type=F32).astype(BF16)
                z_ref[half + j * tile:half + (j + 1) * tile, cols] = jnp.dot(
                    minus[:, cols], sc_ref[...], preferred_element_type=F32).astype(BF16)
        for gidx in range(D_GROUPS):
            cols = slice(gidx * LANES, (gidx + 1) * LANES)
            mid_ref[:, cols] = jnp.dot(u_ref[half:half + mid_ref.shape[0], cols], cc_ref[...],
                                       preferred_element_type=F32)

    k_odd = lax.broadcasted_iota(jnp.int32, (tile, 1), 0) % 2
    sign = 1.0 - 2.0 * k_odd.astype(F32)
    o_ref[...] = (jnp.dot(t_ref[...], z_ref[...], preferred_element_type=F32)
                  + sign * mid_ref[0:1, :]).astype(BF16)


def _dft_tables(seq):
    def cs_rows(rows, cols, period):
        ang = ((rows[:, None] * cols[None, :]) % period).astype(F32) * (2.0 * math.pi / period)
        return jnp.cos(ang), jnp.sin(ang)

    def cs(n, cols):
        split = 64
        if n <= split or n % split:
            return cs_rows(jnp.arange(n, dtype=jnp.int32), cols, n)
        ac, as_ = cs_rows(jnp.arange(n // split, dtype=jnp.int32), cols, n // split)
        bc, bs = cs_rows(jnp.arange(split, dtype=jnp.int32), cols, n)
        cos = ac[:, None, :] * bc[None, :, :] - as_[:, None, :] * bs[None, :, :]
        sin = as_[:, None, :] * bc[None, :, :] + ac[:, None, :] * bs[None, :, :]
        return cos.reshape(n, cols.shape[0]), sin.reshape(n, cols.shape[0])
    half = seq // 2
    col = jnp.arange(seq, dtype=jnp.int32)
    ct, st = cs(seq, col % half)
    tmat = jnp.where(col[None, :] < half, ct, -st).astype(BF16)
    cc, sc = cs(D_GROUP_CH, jnp.arange(D_GROUP_CH, dtype=jnp.int32))
    norm = 1.0 / math.sqrt(seq * D_GROUP_CH)
    tile = min(DFT_ROW_TILE, half)
    i = jnp.arange(tile, dtype=jnp.int32)
    flip = ((i[:, None] + i[None, :] == tile) & (i[:, None] > 0)).astype(BF16)
    return (cc * norm).astype(BF16), (sc * norm).astype(BF16), flip, tmat


def _fourier_mix(u, tables, n_batch, seq, row0, name):
    cc, sc, flip, tmat = tables
    tr = flip.shape[0]
    n_r = seq // tr
    mid_rows = 2 * SUBLANES
    return pl.pallas_call(
        functools.partial(_dft_kernel, seq=seq, tile=tr),
        grid=(n_batch, n_r),
        in_specs=[
            pl.BlockSpec((seq, BRANCH_W), lambda b, r: (row0 // seq + b, 0)),
            pl.BlockSpec((LANES, LANES), lambda b, r: (0, 0)),
            pl.BlockSpec((LANES, LANES), lambda b, r: (0, 0)),
            pl.BlockSpec((tr, tr), lambda b, r: (0, 0)),
            pl.BlockSpec((tr, seq), lambda b, r: (r, 0)),
        ],
        out_specs=pl.BlockSpec((tr, BRANCH_W), lambda b, r: (b * n_r + r, 0)),
        out_shape=jax.ShapeDtypeStruct((n_batch * seq, BRANCH_W), BF16),
        scratch_shapes=[pltpu.VMEM((seq, BRANCH_W), BF16), pltpu.VMEM((mid_rows, BRANCH_W), F32)],
        compiler_params=_cparams(("parallel", "arbitrary")),
        name=name,
    )(u, cc, sc, flip, tmat)


def _merge_kernel(*refs, has_ctx, lat_tiles):
    refs = list(refs)
    n_y = sum(2 if c else 1 for c in has_ctx)
    y_refs, (gt_ref, h_ref, mod_ref, wba_ref, wb_ref, wo_ref, o_ref) = refs[:n_y], refs[n_y:]
    is_lat = pl.program_id(0) < lat_tiles
    acc = None
    for j, split in enumerate(has_ctx):
        y = y_refs.pop(0)[...]
        if split:
            y = jnp.where(is_lat, y, y_refs.pop(0)[...])
        w = wba_ref[...] if j == 0 else wb_ref[j]
        term = gt_ref[:, j * D_MODEL:(j + 1) * D_MODEL].astype(F32) * jnp.dot(
            y, w, preferred_element_type=F32)
        acc = term if acc is None else acc + term
    y = jnp.dot(acc.astype(BF16), wo_ref[...], preferred_element_type=F32)
    o_ref[...] = h_ref[...] + mod_ref[2:3, :] * y


def _merge(ys, gates, h, mods, layer, wb_a, wb, wo, n_tiles, lat_tiles, tiles_per_batch, ctx_row):
    tm = TOKEN_TILE
    row = lambda wd: pl.BlockSpec((tm, wd), lambda i: (i, 0))
    y_specs, y_args, has_ctx = [], [], []
    for y in ys:
        split = isinstance(y, tuple)
        has_ctx.append(split)
        if split:
            y_specs += [pl.BlockSpec((tm, BRANCH_W), lambda i: (jnp.minimum(i, lat_tiles - 1), 0)),
                        pl.BlockSpec((tm, BRANCH_W), lambda i: (jnp.maximum(i - lat_tiles, 0), 0))]
            y_args += list(y)
        else:
            y_specs.append(row(BRANCH_W))
            y_args.append(y)
    return pl.pallas_call(
        functools.partial(_merge_kernel, has_ctx=tuple(has_ctx), lat_tiles=lat_tiles),
        grid=(n_tiles,),
        in_specs=y_specs + [
            row(N_BRANCH * D_MODEL), row(D_MODEL),
            _mod_spec(layer, 1, lat_tiles, tiles_per_batch, ctx_row),
            _const_spec((BRANCH_W, D_MODEL)),
            _const_spec((N_BRANCH, BRANCH_W, D_MODEL), (layer,)),
            _const_spec((D_MODEL, D_MODEL), (layer,)),
        ],
        out_specs=row(D_MODEL),
        out_shape=jax.ShapeDtypeStruct((n_tiles * tm, D_MODEL), F32),
        compiler_params=_cparams(("parallel",)),
        name=f"merge_l{layer}",
    )(*y_args, gates, h, mods, wb_a, wb, wo)


def _rope_tables(seq, pad_rows):
    pos = jnp.arange(seq, dtype=jnp.int32)
    row = (pos // GRID_W).astype(F32)
    col = (pos % GRID_W).astype(F32)
    half = HEAD_DIM // 2
    inv = ROPE_BASE ** (-jnp.arange(0, half, 2, dtype=F32) / half)
    ang_r, ang_c = row[:, None] * inv, col[:, None] * inv
    ang = jnp.concatenate([ang_r, ang_r, ang_c, ang_c], axis=-1)
    cos, sin = jnp.cos(ang), jnp.sin(ang)
    lane = jnp.arange(HEAD_DIM)
    sin_s = jnp.where((lane & 16) == 0, -sin, sin)
    cos = jnp.concatenate([cos, jnp.ones((pad_rows, HEAD_DIM), F32)], axis=0)
    sin_s = jnp.concatenate([sin_s, jnp.zeros((pad_rows, HEAD_DIM), F32)], axis=0)
    return jnp.tile(cos, (1, 2)), jnp.tile(sin_s, (1, 2))


def _paired_query_weight(w_in):
    aq = w_in[:, :BRANCH_W].reshape(D_MODEL, A_KV_HEADS, A_GROUP, HEAD_DIM)
    return jnp.transpose(aq, (0, 2, 1, 3)).reshape(D_MODEL, BRANCH_W).astype(BF16)


def kernel(x, c, ctx, c_ctx, ada_w, ada_b, norm_g, ffn_w1, ffn_w3, ffn_w2, w_in, qk_norm_a,
           conv_w, conv_b, conv_ln_g, conv_ln_b, diff_lam, diff_subln_g, w_branch, w_out, final_g):
    n_batch, seq, _ = x.shape
    ctx_len = ctx.shape[1]
    depth = ada_w.shape[0]
    tm = TOKEN_TILE
    lat_rows, ctx_rows = n_batch * seq, n_batch * ctx_len
    assert seq % tm == 0 and ctx_rows % tm == 0 and ctx_len % CONV_TILE == 0
    assert n_batch < COND_ROWS and seq % GRID_W == 0
    lat_tiles, all_tiles = lat_rows // tm, (lat_rows + ctx_rows) // tm
    tiles_per_batch = seq // tm
    ctx_row = n_batch

    cond = jnp.concatenate(
        [c, c_ctx[None, :], jnp.zeros((COND_ROWS - n_batch - 1, D_MODEL), F32)], axis=0)
    mods = _adaln(cond, ada_w, ada_b).reshape(depth, COND_ROWS, 3, 3, D_MODEL)

    cos_t, sin_t = _rope_tables(seq, tm)
    lane = jnp.arange(LANES)
    seg = (lane[:, None] // HEAD_DIM == lane[None, :] // HEAD_DIM).astype(BF16)
    dft_lat = _dft_tables(seq)
    dft_ctx = _dft_tables(ctx_len)
    tile_args = (lat_tiles, tiles_per_batch, ctx_row)

    h, h_ctx = x.reshape(lat_rows, D_MODEL), ctx.reshape(ctx_rows, D_MODEL)
    w1_bf, w3_bf, w2_bf = ffn_w1.astype(BF16), ffn_w3.astype(BF16), ffn_w2.astype(BF16)
    w_in_bf, wb_bf, wo_bf = w_in.astype(BF16), w_branch.astype(BF16), w_out.astype(BF16)
    for l in range(depth):
        update_ctx = l < depth - 1
        lam_init = 0.8 - 0.6 * math.exp(-0.3 * l)
        out_tiles = all_tiles if update_ctx else lat_tiles
        wb_a = wb_bf[l, 0].reshape(A_KV_HEADS, A_GROUP, HEAD_DIM, D_MODEL)
        wb_a = jnp.transpose(wb_a, (1, 0, 2, 3)).reshape(BRANCH_W, D_MODEL)
        qkg = jnp.tile(qk_norm_a[l], (1, 2))

        h = _half_ffn(h, mods, l, 0, norm_g[l, 0], w1_bf, w3_bf, w2_bf, (l, 0),
                      all_tiles, *tile_args, h_ctx=h_ctx)
        h_ctx = None
        qa, ka, va, yg, qc, kc, vc, du, gates = _in_proj(
            h, mods, l, norm_g[l, 1], w_in_bf, _paired_query_weight(w_in[l]), cos_t, sin_t,
            seg, qkg, all_tiles, *tile_args)

        lat_src = ((0, seq), (lat_rows, ctx_len))
        ctx_src = ((lat_rows, ctx_len),)
        diff_extra = (diff_lam[l], diff_subln_g[l].reshape(1, LANES))
        conv_params = (conv_w[l], conv_b[l], conv_ln_g[l], conv_ln_b[l])
        y_a, y_b = _attention("gqa", qa, ka, va, None, n_batch, seq, 0, lat_src,
                              name=f"gqa_l{l}", conv=(yg, conv_params))
        y_c = _attention("diff", qc, kc, vc, diff_extra, n_batch, seq, 0, lat_src,
                         lam_init=lam_init, name=f"diff_l{l}")
        y_d = _fourier_mix(du, dft_lat, n_batch, seq, 0, f"dft_l{l}")
        if update_ctx:
            y_b = (y_b, _conv_module(yg, *conv_params, lat_rows, ctx_rows, ctx_len))
            y_a = (y_a, _attention("gqa", qa, ka, va, None, n_batch, ctx_len, lat_rows, ctx_src,
                                   name=f"gqa_ctx_l{l}"))
            y_c = (y_c, _attention("diff", qc, kc, vc, diff_extra, n_batch, ctx_len, lat_rows,
                                   ctx_src, lam_init=lam_init, name=f"diff_ctx_l{l}"))
            y_d = (y_d, _fourier_mix(du, dft_ctx, n_batch, ctx_len, lat_rows, f"dft_ctx_l{l}"))
        h = _merge((y_a, y_b, y_c, y_d), gates, h, mods, l, wb_a, wb_bf, wo_bf,
                   out_tiles, *tile_args)
        h = _half_ffn(h, mods, l, 2, norm_g[l, 2], w1_bf, w3_bf, w2_bf, (l, 1),
                      out_tiles, *tile_args, final_g=None if update_ctx else final_g)
    return h.reshape(n_batch, seq, D_MODEL)
```
